```python
import jax, jax.numpy as jnp
from jax import lax
import numpy as np

D_MODEL = 1024
BATCH = 8
SEQ = 2048
DEPTH = 4
DEC_BATCH = 128
DEC_SEQ = 1
PAST_LEN = 16384
PAGE_SIZE = 128

CONV_DIM = D_MODEL
CONV_WIDTH = 31
HG_HEADS = 8
HG_EXPAND = D_MODEL // HG_HEADS
HG_DIM = HG_HEADS * HG_EXPAND
HG_VDIM = D_MODEL
HG_HEAD_V = HG_VDIM // HG_HEADS
CHUNK = 64
LN_EPS = 1e-5
G_FLOOR = 1e-30
ALPHA = (2 * DEPTH) ** 0.25
BETA = (8 * DEPTH) ** -0.25
IN_COLS = 3 * CONV_DIM + 2 * HG_DIM + 2 * HG_VDIM + 2 * D_MODEL

kernel_name = "hybrid_conformer_hgrn2_gated_deepnorm_step"


def _split_points():
    sizes = (CONV_DIM, CONV_DIM, CONV_DIM, HG_DIM, HG_DIM, HG_VDIM, HG_VDIM, D_MODEL, D_MODEL)
    return [int(v) for v in np.cumsum(sizes)[:-1]]


def layer_norm(x, g, b):
    xf = x.astype(jnp.float32)
    mu = jnp.mean(xf, axis=-1, keepdims=True)
    var = jnp.mean(jnp.square(xf - mu), axis=-1, keepdims=True)
    y = (xf - mu) * lax.rsqrt(var + LN_EPS) * g.astype(jnp.float32) + b.astype(jnp.float32)
    return y.astype(x.dtype)


def hgrn2_chunked(q, k, log_g, v, s0):
    B, T, H, DK = q.shape
    DV = v.shape[-1]
    C = CHUNK if T % CHUNK == 0 else T
    n = T // C

    def to_chunks(a):
        return a.reshape(B, n, C, H, a.shape[-1]).transpose(1, 0, 3, 2, 4)

    causal = jnp.tril(jnp.ones((C, C), dtype=bool))[:, :, None]

    def step(s, inp):
        qc, kc, gc, vc = inp
        G = jnp.cumsum(gc, axis=2)
        diff = G[:, :, :, None, :] - G[:, :, None, :, :]
        decay = jnp.where(causal, jnp.exp(jnp.where(causal, diff, 0.0)), 0.0)
        scores = jnp.sum(qc[:, :, :, None, :] * kc[:, :, None, :, :] * decay, axis=-1)
        o = (jnp.einsum('bhij,bhjv->bhiv', scores, vc)
             + jnp.einsum('bhid,bhdv->bhiv', qc * jnp.exp(G), s))
        G_last = G[:, :, -1:, :]
        s_new = (jnp.exp(G_last[:, :, 0, :])[..., None] * s
                 + jnp.einsum('bhjd,bhjv->bhdv', kc * jnp.exp(G_last - G), vc))
        return s_new, o

    s_fin, o = lax.scan(step, s0, (to_chunks(q), to_chunks(k), to_chunks(log_g), to_chunks(v)))
    o = o.transpose(1, 0, 3, 2, 4).reshape(B, T, H, DV)
    return o, s_fin


def mixer_layer(x, conv_buf, s0, lb, w_in, b_in, conv_w, conv_b, cln_g, cln_b, w_cp,
                hg_g, w_hp, w_out, ln_g, ln_b):
    B, T, _ = x.shape
    h = jnp.einsum('btd,dc->btc', x, w_in) + b_in
    glu_a, glu_b, z_c, q, f, i_in, z_h, m_c, m_h = jnp.split(h, _split_points(), axis=-1)

    u = glu_a * jax.nn.sigmoid(glu_b)
    u_ext = jnp.concatenate([conv_buf.astype(u.dtype), u], axis=1)
    new_buf = u_ext[:, -(CONV_WIDTH - 1):]
    c = lax.conv_general_dilated(u_ext, conv_w[:, None, :].astype(u.dtype), window_strides=(1,),
                                 padding='VALID', dimension_numbers=('NWC', 'WIO', 'NWC'),
                                 feature_group_count=CONV_DIM) + conv_b
    c = jax.nn.silu(layer_norm(c, cln_g, cln_b)) * jax.nn.silu(z_c)
    y_c = jnp.einsum('btc,cd->btd', c, w_cp)

    qf = jax.nn.silu(q.astype(jnp.float32)).reshape(B, T, HG_HEADS, HG_EXPAND)
    ff = f.astype(jnp.float32).reshape(B, T, HG_HEADS, HG_EXPAND)
    lbh = lb.astype(jnp.float32).reshape(HG_HEADS, HG_EXPAND)
    g = lbh + (1.0 - lbh) * jax.nn.sigmoid(ff)
    log_g = jnp.log(jnp.maximum(g, G_FLOOR))
    kf = 1.0 - g
    vf = i_in.astype(jnp.float32).reshape(B, T, HG_HEADS, HG_HEAD_V)
    o, s_new = hgrn2_chunked(qf, kf, log_g, vf, s0.astype(jnp.float32))
    o = o * lax.rsqrt(jnp.mean(jnp.square(o), axis=-1, keepdims=True) + LN_EPS) * hg_g.astype(jnp.float32)
    o = o.reshape(B, T, HG_VDIM).astype(x.dtype) * jax.nn.silu(z_h)
    y_h = jnp.einsum('btc,cd->btd', o, w_hp)

    merged = jax.nn.sigmoid(m_c) * y_c + jax.nn.sigmoid(m_h) * y_h
    out = jnp.einsum('btd,de->bte', merged, w_out)
    x_new = layer_norm(ALPHA * x + out, ln_g, ln_b)
    return x_new, new_buf, s_new.astype(x.dtype)


def setup_inputs(seed: int = 0) -> dict:
    key = jax.random.key(seed)
    ks = jax.random.split(key, 18)
    f32 = jnp.float32
    nrm = lambda k, shape, s: jax.random.normal(k, shape, f32) * s
    return {
        "x_prompt": nrm(ks[0], (BATCH, SEQ, D_MODEL), 1.0),
        "x_sample": nrm(ks[1], (DEC_BATCH, DEC_SEQ, D_MODEL), 1.0),
        "state_conv": nrm(ks[2], (DEPTH, DEC_BATCH, CONV_WIDTH - 1, CONV_DIM), 0.5),
        "state_hgrn": nrm(ks[3], (DEPTH, DEC_BATCH, HG_HEADS, HG_EXPAND, HG_HEAD_V), 0.5),
        "w_in": nrm(ks[4], (DEPTH, D_MODEL, IN_COLS), D_MODEL ** -0.5),
        "b_in": nrm(ks[5], (DEPTH, IN_COLS), 0.01),
        "conv_w": nrm(ks[6], (DEPTH, CONV_WIDTH, CONV_DIM), CONV_WIDTH ** -0.5),
        "conv_b": nrm(ks[7], (DEPTH, CONV_DIM), 0.01),
        "conv_ln_g": 1.0 + nrm(ks[8], (DEPTH, CONV_DIM), 0.01),
        "conv_ln_b": nrm(ks[9], (DEPTH, CONV_DIM), 0.01),
        "w_conv_proj": nrm(ks[10], (DEPTH, CONV_DIM, D_MODEL), BETA * CONV_DIM ** -0.5),
        "lower_bounds": nrm(ks[11], (DEPTH, HG_DIM), 0.1),
        "hg_norm_g": 1.0 + nrm(ks[12], (DEPTH, HG_HEAD_V), 0.01),
        "w_hg_proj": nrm(ks[13], (DEPTH, HG_VDIM, D_MODEL), BETA * HG_VDIM ** -0.5),
        "w_out": nrm(ks[14], (DEPTH, D_MODEL, D_MODEL), BETA * D_MODEL ** -0.5),
        "ln_g": 1.0 + nrm(ks[15], (DEPTH, D_MODEL), 0.01),
        "ln_b": nrm(ks[16], (DEPTH, D_MODEL), 0.01),
    }


def reference(x_prompt, x_sample, state_conv, state_hgrn, w_in, b_in, conv_w, conv_b,
              conv_ln_g, conv_ln_b, w_conv_proj, lower_bounds, hg_norm_g, w_hg_proj,
              w_out, ln_g, ln_b):
    p = jax.nn.softmax(lower_bounds.astype(jnp.float32), axis=0)
    lbs = jnp.cumsum(p, axis=0) - p[0]

    xp, xs = x_prompt, x_sample
    conv_p, hgrn_p, conv_s, hgrn_s = [], [], [], []
    zero_buf = jnp.zeros((x_prompt.shape[0], CONV_WIDTH - 1, CONV_DIM), x_prompt.dtype)
    zero_s = jnp.zeros((x_prompt.shape[0], HG_HEADS, HG_EXPAND, HG_HEAD_V), jnp.float32)
    for l in range(DEPTH):
        params = (lbs[l], w_in[l], b_in[l], conv_w[l], conv_b[l], conv_ln_g[l], conv_ln_b[l],
                  w_conv_proj[l], hg_norm_g[l], w_hg_proj[l], w_out[l], ln_g[l], ln_b[l])
        xp, bp, sp = mixer_layer(xp, zero_buf, zero_s, *params)
        xs, bs, ss = mixer_layer(xs, state_conv[l], state_hgrn[l], *params)
        conv_p.append(bp); hgrn_p.append(sp); conv_s.append(bs); hgrn_s.append(ss)
    return (xp, xs, jnp.stack(conv_p), jnp.stack(hgrn_p), jnp.stack(conv_s), jnp.stack(hgrn_s))
```

```python
import functools

import jax
import jax.numpy as jnp
from jax import lax
from jax.experimental import pallas as pl
from jax.experimental.pallas import tpu as pltpu

F32 = jnp.float32
BF16 = jnp.bfloat16

D_MODEL = 1024
DEPTH = 4
CONV_WIDTH = 31
N_HEADS = 8
HEAD_DIM = 128
IN_COLS = 9 * D_MODEL
LN_EPS = 1e-5
G_FLOOR = 1e-30
ALPHA = (2 * DEPTH) ** 0.25

C_GLU_A, C_GLU_B, C_ZC, C_Q, C_F, C_I, C_ZH, C_MC, C_MH = (i * D_MODEL for i in range(9))

HALO = 32
HALO_SKIP = HALO - (CONV_WIDTH - 1)
TIME_TILE = 512
ROW_CHUNK = 128
CONV_LANES = 128
HG_CHUNK = 64
DEC_BLOCK = 8
VMEM_LIMIT_BYTES = 58 * 1024 * 1024


def _dot(a, b):
    return jnp.dot(a, b, preferred_element_type=F32)


def _dot_nt(a, b):
    return lax.dot_general(a, b, (((1,), (1,)), ((), ())), preferred_element_type=F32)


def _dot_tn(a, b):
    return lax.dot_general(a, b, (((0,), (0,)), ((), ())), preferred_element_type=F32)


def _sigmoid(x):
    return 1.0 / (1.0 + jnp.exp(-x))


def _silu(x):
    return x * _sigmoid(x)


def _layer_norm(x, g, b):
    mu = jnp.mean(x, axis=-1, keepdims=True)
    d = x - mu
    var = jnp.mean(d * d, axis=-1, keepdims=True)
    return d * lax.rsqrt(var + LN_EPS) * g + b


def _lower_bounds(lb_all):
    m = jnp.max(lb_all, axis=0, keepdims=True)
    e = jnp.exp(lb_all - m)
    p = e / jnp.sum(e, axis=0, keepdims=True)
    out, run = [], jnp.zeros_like(p[0:1])
    for l in range(DEPTH):
        run = run + p[l:l + 1]
        out.append(run - p[0:1])
    return out


def _rows(i, n):
    return pl.ds(pl.multiple_of(i * n, n), n)


def _prompt_layer_kernel(x_ref, win_ref, bin_ref, cw_ref, cb_ref, clg_ref, clb_ref, wcp_ref,
                         lb_ref, hgg_ref, whp_ref, wout_ref, lng_ref, lnb_ref,
                         y_ref, convst_ref, st_ref,
                         xb_scr, hbuf, uext, yc_scr, act, st_t, *, layer, tb, nt):
    t = pl.program_id(1)
    n_row_chunks = tb // ROW_CHUNK

    @pl.when(t == 0)
    def _():
        uext[0:HALO, :] = jnp.zeros((HALO, D_MODEL), F32)
        st_t[...] = jnp.zeros_like(st_t)

    def xb_body(i, c):
        r = _rows(i, ROW_CHUNK)
        xb_scr[r, :] = x_ref[0, r, :].astype(BF16)
        return c
    lax.fori_loop(0, n_row_chunks, xb_body, 0)

    hbuf[:, 0:3 * D_MODEL] = (_dot(xb_scr[...], win_ref[0, :, C_GLU_A:C_Q])
                              + bin_ref[0, :, C_GLU_A:C_Q])

    def u_body(i, c):
        r = _rows(i, ROW_CHUNK)
        a = hbuf[r, 0:D_MODEL]
        b = hbuf[r, D_MODEL:2 * D_MODEL]
        uext[pl.ds(pl.multiple_of(HALO + i * ROW_CHUNK, 8), ROW_CHUNK), :] = a * _sigmoid(b)
        return c
    lax.fori_loop(0, n_row_chunks, u_body, 0)

    for i in range(n_row_chunks):
        for j in range(D_MODEL // CONV_LANES):
            ls = slice(j * CONV_LANES, (j + 1) * CONV_LANES)
            acc = jnp.broadcast_to(cb_ref[0, :, ls], (ROW_CHUNK, CONV_LANES))
            for w in range(CONV_WIDTH):
                r0 = i * ROW_CHUNK + HALO_SKIP + w
                acc = acc + uext[r0:r0 + ROW_CHUNK, ls] * cw_ref[0, w:w + 1, ls]
            hbuf[i * ROW_CHUNK:(i + 1) * ROW_CHUNK, 3 * D_MODEL + j * CONV_LANES:
                 3 * D_MODEL + (j + 1) * CONV_LANES] = acc

    @pl.when(t == nt - 1)
    def _():
        convst_ref[0] = uext[tb + HALO_SKIP:tb + HALO, :]

    uext[0:HALO, :] = uext[tb:tb + HALO, :]

    def c_body(i, c):
        r = _rows(i, ROW_CHUNK)
        cn = _layer_norm(hbuf[r, 3 * D_MODEL:4 * D_MODEL], clg_ref[0], clb_ref[0])
        z = hbuf[r, 2 * D_MODEL:3 * D_MODEL]
        act[r, :] = (_silu(cn) * _silu(z)).astype(BF16)
        return c
    lax.fori_loop(0, n_row_chunks, c_body, 0)

    yc_scr[...] = _dot(act[...], wcp_ref[0])

    hbuf[...] = _dot(xb_scr[...], win_ref[0, :, C_Q:C_MC]) + bin_ref[0, :, C_Q:C_MC]

    lb = _lower_bounds(lb_ref[...])[layer]
    ri = lax.broadcasted_iota(jnp.int32, (HG_CHUNK, HG_CHUNK), 0)
    ci = lax.broadcasted_iota(jnp.int32, (HG_CHUNK, HG_CHUNK), 1)
    causal = ri >= ci
    tri = causal.astype(F32)
    mid = HG_CHUNK // 2 - 1
    hgg = hgg_ref[0]

    def hg_body(c, carry):
        r = _rows(c, HG_CHUNK)
        q = hbuf[r, 0:D_MODEL]
        f = hbuf[r, D_MODEL:2 * D_MODEL]
        qs = _silu(q)
        g = lb + (1.0 - lb) * _sigmoid(f)
        lg = jnp.log(jnp.maximum(g, G_FLOOR))
        k = 1.0 - g
        gc = jnp.dot(tri, lg, preferred_element_type=F32, precision=lax.Precision.HIGHEST)
        g_mid = gc[mid:mid + 1, :]
        g_last = gc[HG_CHUNK - 1:HG_CHUNK, :]
        qt = qs * jnp.exp(gc - g_mid)
        kt = k * jnp.exp(g_mid - gc)
        qh = (qt * jnp.exp(g_mid)).astype(BF16)
        kh = (kt * jnp.exp(g_last - g_mid)).astype(BF16)
        qt = qt.astype(BF16)
        kt = kt.astype(BF16)
        dec = jnp.exp(g_last)
        vb = hbuf[r, 2 * D_MODEL:3 * D_MODEL].astype(BF16)
        for h in range(N_HEADS):
            sl = slice(h * HEAD_DIM, (h + 1) * HEAD_DIM)
            a = jnp.where(causal, _dot_nt(qt[:, sl], kt[:, sl]), 0.0)
            s_t = st_t[h]
            o = _dot(a.astype(BF16), vb[:, sl]) + _dot_nt(qh[:, sl], s_t.astype(BF16))
            st_t[h] = s_t * dec[:, sl] + _dot_tn(vb[:, sl], kh[:, sl])
            ms = jnp.mean(o * o, axis=-1, keepdims=True)
            on = o * lax.rsqrt(ms + LN_EPS) * hgg
            z = hbuf[r, 3 * D_MODEL + h * HEAD_DIM:3 * D_MODEL + (h + 1) * HEAD_DIM]
            act[r, sl] = (on * _silu(z)).astype(BF16)
        return carry
    lax.fori_loop(0, tb // HG_CHUNK, hg_body, 0)

    @pl.when(t == nt - 1)
    def _():
        for h in range(N_HEADS):
            st_ref[0, h] = st_t[h].T

    hbuf[:, 2 * D_MODEL:3 * D_MODEL] = _dot(act[...], whp_ref[0])
    hbuf[:, 0:2 * D_MODEL] = (_dot(xb_scr[...], win_ref[0, :, C_MC:IN_COLS])
                              + bin_ref[0, :, C_MC:IN_COLS])

    def m_body(i, c):
        r = _rows(i, ROW_CHUNK)
        mc = _sigmoid(hbuf[r, 0:D_MODEL])
        mh = _sigmoid(hbuf[r, D_MODEL:2 * D_MODEL])
        act[r, :] = (mc * yc_scr[r, :] + mh * hbuf[r, 2 * D_MODEL:3 * D_MODEL]).astype(BF16)
        return c
    lax.fori_loop(0, n_row_chunks, m_body, 0)

    hbuf[:, 3 * D_MODEL:4 * D_MODEL] = _dot(act[...], wout_ref[0])

    def o_body(i, c):
        r = _rows(i, ROW_CHUNK)
        res = ALPHA * x_ref[0, r, :] + hbuf[r, 3 * D_MODEL:4 * D_MODEL]
        y_ref[0, r, :] = _layer_norm(res, lng_ref[0], lnb_ref[0])
        return c
    lax.fori_loop(0, n_row_chunks, o_body, 0)


def _resident(shape, index_map):
    return pl.BlockSpec(shape, index_map, pipeline_mode=pl.Buffered(1))


def _prompt_layer(layer, x, p):
    B, T, _ = x.shape
    tb = min(TIME_TILE, T)
    nt = T // tb
    l = layer
    vec = lambda n: _resident((1, 1, n), lambda b, t: (l, 0, 0))
    in_specs = [
        pl.BlockSpec((1, tb, D_MODEL), lambda b, t: (b, t, 0)),
        _resident((1, D_MODEL, IN_COLS), lambda b, t: (l, 0, 0)),
        vec(IN_COLS),
        _resident((1, CONV_WIDTH, D_MODEL), lambda b, t: (l, 0, 0)),
        vec(D_MODEL), vec(D_MODEL), vec(D_MODEL),
        _resident((1, D_MODEL, D_MODEL), lambda b, t: (l, 0, 0)),
        _resident((DEPTH, D_MODEL), lambda b, t: (0, 0)),
        vec(HEAD_DIM),
        _resident((1, D_MODEL, D_MODEL), lambda b, t: (l, 0, 0)),
        _resident((1, D_MODEL, D_MODEL), lambda b, t: (l, 0, 0)),
        vec(D_MODEL), vec(D_MODEL),
    ]
    out_specs = [
        pl.BlockSpec((1, tb, D_MODEL), lambda b, t: (b, t, 0)),
        pl.BlockSpec((1, CONV_WIDTH - 1, D_MODEL), lambda b, t: (b, 0, 0)),
        pl.BlockSpec((1, N_HEADS, HEAD_DIM, HEAD_DIM), lambda b, t: (b, 0, 0, 0)),
    ]
    out_shape = [
        jax.ShapeDtypeStruct((B, T, D_MODEL), F32),
        jax.ShapeDtypeStruct((B, CONV_WIDTH - 1, D_MODEL), F32),
        jax.ShapeDtypeStruct((B, N_HEADS, HEAD_DIM, HEAD_DIM), F32),
    ]
    scratch = [
        pltpu.VMEM((tb, D_MODEL), BF16),
        pltpu.VMEM((tb, 4 * D_MODEL), F32),
        pltpu.VMEM((tb + HALO, D_MODEL), F32),
        pltpu.VMEM((tb, D_MODEL), F32),
        pltpu.VMEM((tb, D_MODEL), BF16),
        pltpu.VMEM((N_HEADS, HEAD_DIM, HEAD_DIM), F32),
    ]
    kern = functools.partial(_prompt_layer_kernel, layer=layer, tb=tb, nt=nt)
    return pl.pallas_call(
        kern,
        grid=(B, nt),
        in_specs=in_specs,
        out_specs=out_specs,
        out_shape=out_shape,
        scratch_shapes=scratch,
        compiler_params=pltpu.CompilerParams(
            dimension_semantics=("arbitrary", "arbitrary"),
            vmem_limit_bytes=VMEM_LIMIT_BYTES),
        name=f"prompt_layer_{layer}",
    )(x, p["w_in"], p["b_in"], p["conv_w"], p["conv_b"], p["conv_ln_g"], p["conv_ln_b"],
      p["w_conv_proj"], p["lower_bounds"], p["hg_norm_g"], p["w_hg_proj"], p["w_out"],
      p["ln_g"], p["ln_b"])


def _decode_kernel(x_ref, sc_ref, sh_ref, win_ref, bin_ref, cw_ref, cb_ref, clg_ref, clb_ref,
                   wcp_ref, lb_ref, hgg_ref, whp_ref, wout_ref, lng_ref, lnb_ref,
                   y_ref, sco_ref, sho_ref,
                   xcur, hbuf, lbs_scr, q_t, g_t, k_t, u_scr, o_scr, c_scr, o_stage, c_stage,
                   *, nb, bb):
    l = pl.program_id(0)
    j = pl.program_id(1)
    n = xcur.shape[0]

    @pl.when((l == 0) & (j == 0))
    def _():
        xcur[...] = x_ref[...]
        lbs = _lower_bounds(lb_ref[...])
        lbs_scr[...] = jnp.zeros_like(lbs_scr)
        for i in range(DEPTH):
            lbs_scr[i:i + 1, :] = lbs[i]

    @pl.when(j == 0)
    def _():
        hbuf[...] = _dot(xcur[...].astype(BF16), win_ref[0]) + bin_ref[0]
        u_scr[...] = hbuf[:, C_GLU_A:C_GLU_B] * _sigmoid(hbuf[:, C_GLU_B:C_ZC])
        layer_row = lax.broadcasted_iota(jnp.int32, lbs_scr.shape, 0) == l
        lb = jnp.sum(jnp.where(layer_row, lbs_scr[...], 0.0), axis=0, keepdims=True)
        g = lb + (1.0 - lb) * _sigmoid(hbuf[:, C_F:C_I])
        q_t[...] = _silu(hbuf[:, C_Q:C_F]).T
        g_t[...] = jnp.maximum(g, G_FLOOR).T
        k_t[...] = (1.0 - g).T

    shift = (n - j * bb) % n
    qc = pltpu.roll(q_t[...], shift, 1)
    gc = pltpu.roll(g_t[...], shift, 1)
    kc = pltpu.roll(k_t[...], shift, 1)
    cw_hist = cw_ref[0, 0:CONV_WIDTH - 1, :]
    cw_new = cw_ref[0, CONV_WIDTH - 1:CONV_WIDTH, :]
    rows = _rows(j, bb)
    u_blk = u_scr[rows, :]
    v_blk = hbuf[rows, C_I:C_ZH]
    for b in range(bb):
        u_row = u_blk[b:b + 1, :]
        hist = sc_ref[0, b]
        c_stage[b:b + 1, :] = (jnp.sum(hist * cw_hist, axis=0, keepdims=True)
                               + u_row * cw_new + cb_ref[0])
        sco_ref[0, b, 0:CONV_WIDTH - 2, :] = sc_ref[0, b, 1:CONV_WIDTH - 1, :]
        sco_ref[0, b, CONV_WIDTH - 2:CONV_WIDTH - 1, :] = u_row
        for h in range(N_HEADS):
            hs = slice(h * HEAD_DIM, (h + 1) * HEAD_DIM)
            s_new = gc[hs, b:b + 1] * sh_ref[0, b, h] + kc[hs, b:b + 1] * v_blk[b:b + 1, hs]
            sho_ref[0, b, h] = s_new
            o_stage[b:b + 1, hs] = jnp.sum(qc[hs, b:b + 1] * s_new, axis=0, keepdims=True)
    c_scr[rows, :] = c_stage[...]
    o_scr[rows, :] = o_stage[...]

    @pl.when(j == nb - 1)
    def _():
        cn = _layer_norm(c_scr[...], clg_ref[0], clb_ref[0])
        ca = (_silu(cn) * _silu(hbuf[:, C_ZC:C_Q])).astype(BF16)
        yc = _dot(ca, wcp_ref[0])
        for h in range(N_HEADS):
            hs = slice(h * HEAD_DIM, (h + 1) * HEAD_DIM)
            o = o_scr[:, hs]
            ms = jnp.mean(o * o, axis=-1, keepdims=True)
            on = o * lax.rsqrt(ms + LN_EPS) * hgg_ref[0]
            o_scr[:, hs] = on * _silu(hbuf[:, C_ZH + h * HEAD_DIM:C_ZH + (h + 1) * HEAD_DIM])
        yh = _dot(o_scr[...].astype(BF16), whp_ref[0])
        merged = _sigmoid(hbuf[:, C_MC:C_MH]) * yc + _sigmoid(hbuf[:, C_MH:IN_COLS]) * yh
        out = _dot(merged.astype(BF16), wout_ref[0])
        x_new = _layer_norm(ALPHA * xcur[...] + out, lng_ref[0], lnb_ref[0])
        xcur[...] = x_new
        y_ref[...] = x_new


def _decode(x, state_conv, state_hgrn, p):
    n = x.shape[0]
    bb = DEC_BLOCK
    nb = n // bb
    vec = lambda m: _resident((1, 1, m), lambda l, j: (l, 0, 0))
    in_specs = [
        _resident((n, D_MODEL), lambda l, j: (0, 0)),
        pl.BlockSpec((1, bb, CONV_WIDTH - 1, D_MODEL), lambda l, j: (l, j, 0, 0)),
        pl.BlockSpec((1, bb, N_HEADS, HEAD_DIM, HEAD_DIM), lambda l, j: (l, j, 0, 0, 0)),
        _resident((1, D_MODEL, IN_COLS), lambda l, j: (l, 0, 0)),
        vec(IN_COLS),
        _resident((1, CONV_WIDTH, D_MODEL), lambda l, j: (l, 0, 0)),
        vec(D_MODEL), vec(D_MODEL), vec(D_MODEL),
        _resident((1, D_MODEL, D_MODEL), lambda l, j: (l, 0, 0)),
        _resident((DEPTH, D_MODEL), lambda l, j: (0, 0)),
        vec(HEAD_DIM),
        _resident((1, D_MODEL, D_MODEL), lambda l, j: (l, 0, 0)),
        _resident((1, D_MODEL, D_MODEL), lambda l, j: (l, 0, 0)),
        vec(D_MODEL), vec(D_MODEL),
    ]
    out_specs = [
        pl.BlockSpec((n, D_MODEL), lambda l, j: (0, 0)),
        pl.BlockSpec((1, bb, CONV_WIDTH - 1, D_MODEL), lambda l, j: (l, j, 0, 0)),
        pl.BlockSpec((1, bb, N_HEADS, HEAD_DIM, HEAD_DIM), lambda l, j: (l, j, 0, 0, 0)),
    ]
    out_shape = [
        jax.ShapeDtypeStruct((n, D_MODEL), F32),
        jax.ShapeDtypeStruct(state_conv.shape, F32),
        jax.ShapeDtypeStruct(state_hgrn.shape, F32),
    ]
    scratch = [
        pltpu.VMEM((n, D_MODEL), F32),
        pltpu.VMEM((n, IN_COLS), F32),
        pltpu.VMEM((8, D_MODEL), F32),
        pltpu.VMEM((D_MODEL, n), F32),
        pltpu.VMEM((D_MODEL, n), F32),
        pltpu.VMEM((D_MODEL, n), F32),
        pltpu.VMEM((n, D_MODEL), F32),
        pltpu.VMEM((n, D_MODEL), F32),
        pltpu.VMEM((n, D_MODEL), F32),
        pltpu.VMEM((bb, D_MODEL), F32),
        pltpu.VMEM((bb, D_MODEL), F32),
    ]
    kern = functools.partial(_decode_kernel, nb=nb, bb=bb)
    return pl.pallas_call(
        kern,
        grid=(DEPTH, nb),
        in_specs=in_specs,
        out_specs=out_specs,
        out_shape=out_shape,
        scratch_shapes=scratch,
        compiler_params=pltpu.CompilerParams(
            dimension_semantics=("arbitrary", "arbitrary"),
            vmem_limit_bytes=VMEM_LIMIT_BYTES),
        name="decode_layers",
    )(x, state_conv, state_hgrn, p["w_in"], p["b_in"], p["conv_w"], p["conv_b"], p["conv_ln_g"],
      p["conv_ln_b"], p["w_conv_proj"], p["lower_bounds"], p["hg_norm_g"], p["w_hg_proj"],
      p["w_out"], p["ln_g"], p["ln_b"])


def kernel(x_prompt, x_sample, state_conv, state_hgrn, w_in, b_in, conv_w, conv_b, conv_ln_g,
           conv_ln_b, w_conv_proj, lower_bounds, hg_norm_g, w_hg_proj, w_out, ln_g, ln_b):
    row = lambda a: a.reshape(DEPTH, 1, a.shape[-1])
    p = dict(
        w_in=w_in.astype(BF16), b_in=row(b_in), conv_w=conv_w, conv_b=row(conv_b),
        conv_ln_g=row(conv_ln_g), conv_ln_b=row(conv_ln_b), w_conv_proj=w_conv_proj.astype(BF16),
        lower_bounds=lower_bounds, hg_norm_g=row(hg_norm_g), w_hg_proj=w_hg_proj.astype(BF16),
        w_out=w_out.astype(BF16), ln_g=row(ln_g), ln_b=row(ln_b))

    xp = x_prompt
    conv_p, hgrn_p = [], []
    for layer in range(DEPTH):
        xp, cs, hs = _prompt_layer(layer, xp, p)
        conv_p.append(cs)
        hgrn_p.append(hs)

    n = x_sample.shape[0]
    ys, conv_s, hgrn_s = _decode(x_sample.reshape(n, D_MODEL), state_conv, state_hgrn, p)
    return (xp, ys.reshape(x_sample.shape), jnp.stack(conv_p), jnp.stack(hgrn_p), conv_s, hgrn_s)
```

```python
import functools

import jax
import jax.numpy as jnp
from jax import lax
from jax.experimental import pallas as pl
from jax.experimental.pallas import tpu as pltpu

F32 = jnp.float32
BF16 = jnp.bfloat16

D_MODEL = 1024
DEPTH = 4
CONV_WIDTH = 31
N_HEADS = 8
HEAD_DIM = 128
IN_COLS = 9 * D_MODEL
LN_EPS = 1e-5
G_FLOOR = 1e-30
ALPHA = (2 * DEPTH) ** 0.25

C_GLU_A, C_GLU_B, C_ZC, C_Q, C_F, C_I, C_ZH, C_MC, C_MH = (i * D_MODEL for i in range(9))

SUBLANES = 8
LANES = 128
HALO = 32
HALO_SKIP = HALO - (CONV_WIDTH - 1)
CHUNK = 128
PAIR = 2 * CHUNK
SUB = 32
HG_CHUNK = 64
DEC_BLOCK = 8
VMEM_LIMIT_BYTES = 58 * 1024 * 1024


def _dot(a, b):
    return jnp.dot(a, b, preferred_element_type=F32)


def _dot_nt(a, b):
    return lax.dot_general(a, b, (((1,), (1,)), ((), ())), preferred_element_type=F32)


def _dot_tn(a, b):
    return lax.dot_general(a, b, (((0,), (0,)), ((), ())), preferred_element_type=F32)


def _sigmoid(x):
    return 0.5 * jnp.tanh(0.5 * x) + 0.5


def _silu(x):
    h = 0.5 * x
    return h + h * jnp.tanh(h)


def _layer_norm(x, g, b):
    mu = jnp.mean(x, axis=-1, keepdims=True)
    d = x - mu
    var = jnp.mean(d * d, axis=-1, keepdims=True)
    return d * lax.rsqrt(var + LN_EPS) * g + b


def _lower_bounds(lb_all):
    m = jnp.max(lb_all, axis=0, keepdims=True)
    e = jnp.exp(lb_all - m)
    p = e / jnp.sum(e, axis=0, keepdims=True)
    out, run = [], jnp.zeros_like(p[0:1])
    for l in range(DEPTH):
        run = run + p[l:l + 1]
        out.append(run - p[0:1])
    return out


def _rows(i, n):
    return pl.ds(pl.multiple_of(i * n, n), n)


def _conv_block(uext, cw_ref, cb_ref, c_scr, j):
    n_win = CHUNK + HALO
    ls = slice(j * LANES, (j + 1) * LANES)
    win = uext[:, ls]
    acc = jnp.broadcast_to(cb_ref[0, :, ls], (CHUNK, LANES))
    for s in range(SUBLANES):
        sh = win if s == 0 else pltpu.roll(win, n_win - s, 0)
        for a in range(HALO // SUBLANES + 1):
            w = SUBLANES * a + s - HALO_SKIP
            if 0 <= w < CONV_WIDTH:
                acc = acc + sh[SUBLANES * a:SUBLANES * a + CHUNK] * cw_ref[0, w:w + 1, ls]
    c_scr[:, ls] = acc


def _hgrn_chunk(h_r, r, c0, c1, hgg, tri3, causal, st_t, acth_w, reset):
    mid = HG_CHUNK // 2 - 1
    qs = _silu(h_r[r, C_Q:C_F])
    pt = c1 * jnp.tanh(0.5 * h_r[r, C_F:C_I])
    k = c1 - pt
    lg = jnp.log(jnp.maximum(c0 + pt, G_FLOOR))
    hi = lg.astype(BF16)
    r1 = lg - hi.astype(F32)
    md = r1.astype(BF16)
    lo = (r1 - md.astype(F32)).astype(BF16)
    gc = _dot(tri3, jnp.concatenate([hi, md, lo], axis=0))
    g_mid = gc[mid:mid + 1, :]
    g_last = gc[HG_CHUNK - 1:HG_CHUNK, :]
    qt = qs * jnp.exp(gc - g_mid)
    kt = k * jnp.exp(g_mid - gc)
    qh = (qt * jnp.exp(g_mid)).astype(BF16)
    kh = (kt * jnp.exp(g_last - g_mid)).astype(BF16)
    qt = qt.astype(BF16)
    kt = kt.astype(BF16)
    dec = jnp.exp(g_last)
    v = h_r[r, C_I:C_ZH]
    vb = v.astype(BF16)
    for h in range(N_HEADS):
        sl = slice(h * HEAD_DIM, (h + 1) * HEAD_DIM)
        a = jnp.where(causal, _dot_nt(qt[:, sl], kt[:, sl]), 0.0)
        s_t = st_t[h]
        if reset is not None:
            s_t = jnp.where(reset, 0.0, s_t)
        o = _dot(a.astype(BF16), vb[:, sl]) + _dot_nt(qh[:, sl], s_t.astype(BF16))
        v_t = v[:, sl].T.astype(BF16)
        st_t[h] = s_t * dec[:, sl] + _dot(v_t, kh[:, sl])
        ms = jnp.mean(o * o, axis=-1, keepdims=True)
        on = o * lax.rsqrt(ms + LN_EPS) * hgg
        z = h_r[r, C_ZH + h * HEAD_DIM:C_ZH + (h + 1) * HEAD_DIM]
        acth_w[r, sl] = (on * _silu(z)).astype(BF16)


def _prompt_layer_kernel(xn_ref, xp_ref, win_ref, bin_ref, cw_ref, cb_ref, clg_ref, clb_ref, wcp_ref,
                         lb_ref, hgg_ref, whp_ref, wout_ref, lng_ref, lnb_ref,
                         y_ref, convst_ref, st_ref,
                         h0, h1, actc0, actc1, acth0, acth1, uext, c_scr, g_scr, yc_scr, yh_scr,
                         m_scr, o_scr, st_t, xb_scr, xpb_scr, *, layer, pairs_per_row):
    s = pl.program_id(0)

    @pl.when(s == 0)
    def _():
        for ref in (h0, h1, actc0, actc1, acth0, acth1, uext, st_t):
            ref[...] = jnp.zeros_like(ref)

    lb = _lower_bounds(lb_ref[...])[layer]
    c0 = 0.5 + 0.5 * lb
    c1 = 0.5 - 0.5 * lb
    hgg = hgg_ref[0]
    ri = lax.broadcasted_iota(jnp.int32, (HG_CHUNK, HG_CHUNK), 0)
    ci = lax.broadcasted_iota(jnp.int32, (HG_CHUNK, HG_CHUNK), 1)
    causal = ri >= ci
    ri3 = lax.broadcasted_iota(jnp.int32, (HG_CHUNK, 3 * HG_CHUNK), 0)
    ci3 = lax.broadcasted_iota(jnp.int32, (HG_CHUNK, 3 * HG_CHUNK), 1)
    tri3 = (ri3 >= (ci3 & (HG_CHUNK - 1))).astype(BF16)
    n_sub = CHUNK // SUB

    def tick(parity, row_start):
        h_w, h_r = (h0, h1) if parity == 0 else (h1, h0)
        actc_w, acth_w = (actc1, acth1) if parity == 0 else (actc0, acth0)
        actc_r, acth_r = (actc0, acth0) if parity == 0 else (actc1, acth1)
        rows = slice(parity * CHUNK, (parity + 1) * CHUNK)

        sub = lambda i: slice(i * SUB, (i + 1) * SUB)
        col = lambda g: slice(g * D_MODEL, (g + 1) * D_MODEL)

        def in_proj(g):
            h_w[:, col(g)] = _dot(xb_scr[...], win_ref[0, :, col(g)]) + bin_ref[0, :, col(g)]

        def glu(i):
            uext[HALO + i * SUB:HALO + (i + 1) * SUB, :] = (
                h_r[sub(i), C_GLU_A:C_GLU_B] * _sigmoid(h_r[sub(i), C_GLU_B:C_ZC]))

        def conv_ln(i):
            cn = _layer_norm(c_scr[sub(i), :], clg_ref[0], clb_ref[0])
            actc_w[sub(i), :] = (_silu(cn) * _silu(h_r[sub(i), C_ZC:C_Q])).astype(BF16)

        def hgrn(c):
            r = slice(c * HG_CHUNK, (c + 1) * HG_CHUNK)
            _hgrn_chunk(h_r, r, c0, c1, hgg, tri3, causal, st_t, acth_w,
                        row_start if c == 0 else None)

        def gate_proj(g):
            cg = slice(C_MC + g * D_MODEL, C_MC + (g + 1) * D_MODEL)
            g_scr[:, col(g)] = _dot(xpb_scr[...], win_ref[0, :, cg]) + bin_ref[0, :, cg]

        def merge(i):
            m_scr[sub(i), :] = (_sigmoid(g_scr[sub(i), col(0)]) * yc_scr[sub(i), :]
                                + _sigmoid(g_scr[sub(i), col(1)]) * yh_scr[sub(i), :]).astype(BF16)

        def post_ln(i):
            rr = slice(parity * CHUNK + i * SUB, parity * CHUNK + (i + 1) * SUB)
            res = ALPHA * xp_ref[0, rr, :] + o_scr[sub(i), :]
            y_ref[0, rr, :] = _layer_norm(res, lng_ref[0], lnb_ref[0])

        xb_scr[...] = xn_ref[0, rows, :].astype(BF16)
        xpb_scr[...] = xp_ref[0, rows, :].astype(BF16)
        if row_start is not None:
            uext[0:HALO, :] = jnp.where(row_start, 0.0, uext[0:HALO, :])
        in_proj(0); glu(0); glu(1)
        in_proj(1); glu(2); glu(3)
        n_lane_blocks = D_MODEL // LANES
        mxu_pieces = ([functools.partial(in_proj, g) for g in range(2, C_MC // D_MODEL)]
                      + [functools.partial(gate_proj, 0), functools.partial(gate_proj, 1)])
        for j in range(n_lane_blocks):
            if j < len(mxu_pieces):
                mxu_pieces[j]()
            _conv_block(uext, cw_ref, cb_ref, c_scr, j)
        for piece in mxu_pieces[n_lane_blocks:]:
            piece()
        uext[0:HALO, :] = uext[CHUNK:CHUNK + HALO, :]
        yc_scr[...] = _dot(actc_r[...], wcp_ref[0])
        conv_ln(0); conv_ln(1)
        yh_scr[...] = _dot(acth_r[...], whp_ref[0])
        conv_ln(2); conv_ln(3)
        hgrn(0)
        for i in range(n_sub):
            merge(i)
        o_scr[...] = _dot(m_scr[...], wout_ref[0])
        hgrn(1)
        for i in range(n_sub):
            post_ln(i)

    tick(0, None)

    @pl.when((s > 0) & (s % pairs_per_row == 0))
    def _():
        convst_ref[0] = uext[HALO_SKIP:HALO, :]
        for h in range(N_HEADS):
            st_ref[0, h] = st_t[h].T

    tick(1, s % pairs_per_row == 0)


def _resident(shape, index_map):
    return pl.BlockSpec(shape, index_map, pipeline_mode=pl.Buffered(1))


def _prompt_layer(layer, x, p):
    B, T, _ = x.shape
    ppr = T // PAIR
    n_pairs = B * ppr
    l = layer

    def next_pair(s):
        q = jnp.minimum(s, n_pairs - 1)
        return (q // ppr, q % ppr, 0)

    def prev_pair(s):
        q = jnp.maximum(s - 1, 0)
        return (q // ppr, q % ppr, 0)

    def state_row(s):
        return jnp.clip((2 * s - 1) // (2 * ppr), 0, B - 1)

    vec = lambda n: _resident((1, 1, n), lambda s: (l, 0, 0))
    in_specs = [
        pl.BlockSpec((1, PAIR, D_MODEL), next_pair),
        pl.BlockSpec((1, PAIR, D_MODEL), prev_pair),
        _resident((1, D_MODEL, IN_COLS), lambda s: (l, 0, 0)),
        vec(IN_COLS),
        _resident((1, CONV_WIDTH, D_MODEL), lambda s: (l, 0, 0)),
        vec(D_MODEL), vec(D_MODEL), vec(D_MODEL),
        _resident((1, D_MODEL, D_MODEL), lambda s: (l, 0, 0)),
        _resident((DEPTH, D_MODEL), lambda s: (0, 0)),
        vec(HEAD_DIM),
        _resident((1, D_MODEL, D_MODEL), lambda s: (l, 0, 0)),
        _resident((1, D_MODEL, D_MODEL), lambda s: (l, 0, 0)),
        vec(D_MODEL), vec(D_MODEL),
    ]
    out_specs = [
        pl.BlockSpec((1, PAIR, D_MODEL), prev_pair),
        pl.BlockSpec((1, CONV_WIDTH - 1, D_MODEL), lambda s: (state_row(s), 0, 0)),
        pl.BlockSpec((1, N_HEADS, HEAD_DIM, HEAD_DIM), lambda s: (state_row(s), 0, 0, 0)),
    ]
    out_shape = [
        jax.ShapeDtypeStruct((B, T, D_MODEL), F32),
        jax.ShapeDtypeStruct((B, CONV_WIDTH - 1, D_MODEL), F32),
        jax.ShapeDtypeStruct((B, N_HEADS, HEAD_DIM, HEAD_DIM), F32),
    ]
    scratch = [
        pltpu.VMEM((CHUNK, C_MC), F32), pltpu.VMEM((CHUNK, C_MC), F32),
        pltpu.VMEM((CHUNK, D_MODEL), BF16), pltpu.VMEM((CHUNK, D_MODEL), BF16),
        pltpu.VMEM((CHUNK, D_MODEL), BF16), pltpu.VMEM((CHUNK, D_MODEL), BF16),
        pltpu.VMEM((CHUNK + HALO, D_MODEL), F32),
        pltpu.VMEM((CHUNK, D_MODEL), F32),
        pltpu.VMEM((CHUNK, 2 * D_MODEL), F32),
        pltpu.VMEM((CHUNK, D_MODEL), F32), pltpu.VMEM((CHUNK, D_MODEL), F32),
        pltpu.VMEM((CHUNK, D_MODEL), BF16),
        pltpu.VMEM((CHUNK, D_MODEL), F32),
        pltpu.VMEM((N_HEADS, HEAD_DIM, HEAD_DIM), F32),
        pltpu.VMEM((CHUNK, D_MODEL), BF16), pltpu.VMEM((CHUNK, D_MODEL), BF16),
    ]
    kern = functools.partial(_prompt_layer_kernel, layer=layer, pairs_per_row=ppr)
    return pl.pallas_call(
        kern,
        grid=(n_pairs + 1,),
        in_specs=in_specs,
        out_specs=out_specs,
        out_shape=out_shape,
        scratch_shapes=scratch,
        compiler_params=pltpu.CompilerParams(
            dimension_semantics=("arbitrary",),
            vmem_limit_bytes=VMEM_LIMIT_BYTES),
        name=f"prompt_layer_{layer}",
    )(x, x, p["w_in"], p["b_in"], p["conv_w"], p["conv_b"], p["conv_ln_g"], p["conv_ln_b"],
      p["w_conv_proj"], p["lower_bounds"], p["hg_norm_g"], p["w_hg_proj"], p["w_out"],
      p["ln_g"], p["ln_b"])


def _decode_kernel(x_ref, sc_ref, sh_ref, win_ref, bin_ref, cw_ref, cb_ref, clg_ref, clb_ref,
                   wcp_ref, lb_ref, hgg_ref, whp_ref, wout_ref, lng_ref, lnb_ref,
                   y_ref, sco_ref, sho_ref,
                   xcur, hbuf, lbs_scr, q_t, g_t, k_t, u_scr, o_scr, c_scr, o_stage, c_stage,
                   *, nb, bb):
    l = pl.program_id(0)
    j = pl.program_id(1)
    n = xcur.shape[0]

    @pl.when((l == 0) & (j == 0))
    def _():
        xcur[...] = x_ref[...]
        lbs = _lower_bounds(lb_ref[...])
        lbs_scr[...] = jnp.zeros_like(lbs_scr)
        for i in range(DEPTH):
            lbs_scr[i:i + 1, :] = lbs[i]

    @pl.when(j == 0)
    def _():
        hbuf[...] = _dot(xcur[...].astype(BF16), win_ref[0]) + bin_ref[0]
        u_scr[...] = hbuf[:, C_GLU_A:C_GLU_B] * _sigmoid(hbuf[:, C_GLU_B:C_ZC])
        layer_row = lax.broadcasted_iota(jnp.int32, lbs_scr.shape, 0) == l
        lb = jnp.sum(jnp.where(layer_row, lbs_scr[...], 0.0), axis=0, keepdims=True)
        g = lb + (1.0 - lb) * _sigmoid(hbuf[:, C_F:C_I])
        q_t[...] = _silu(hbuf[:, C_Q:C_F]).T
        g_t[...] = jnp.maximum(g, G_FLOOR).T
        k_t[...] = (1.0 - g).T

    shift = (n - j * bb) % n
    qc = pltpu.roll(q_t[...], shift, 1)
    gc = pltpu.roll(g_t[...], shift, 1)
    kc = pltpu.roll(k_t[...], shift, 1)
    cw_hist = cw_ref[0, 0:CONV_WIDTH - 1, :]
    cw_new = cw_ref[0, CONV_WIDTH - 1:CONV_WIDTH, :]
    rows = _rows(j, bb)
    u_blk = u_scr[rows, :]
    v_blk = hbuf[rows, C_I:C_ZH]
    for b in range(bb):
        u_row = u_blk[b:b + 1, :]
        hist = sc_ref[0, b]
        c_stage[b:b + 1, :] = (jnp.sum(hist * cw_hist, axis=0, keepdims=True)
                               + u_row * cw_new + cb_ref[0])
        sco_ref[0, b, 0:CONV_WIDTH - 2, :] = sc_ref[0, b, 1:CONV_WIDTH - 1, :]
        sco_ref[0, b, CONV_WIDTH - 2:CONV_WIDTH - 1, :] = u_row
        for h in range(N_HEADS):
            hs = slice(h * HEAD_DIM, (h + 1) * HEAD_DIM)
            s_new = gc[hs, b:b + 1] * sh_ref[0, b, h] + kc[hs, b:b + 1] * v_blk[b:b + 1, hs]
            sho_ref[0, b, h] = s_new
            o_stage[b:b + 1, hs] = jnp.sum(qc[hs, b:b + 1] * s_new, axis=0, keepdims=True)
    c_scr[rows, :] = c_stage[...]
    o_scr[rows, :] = o_stage[...]

    @pl.when(j == nb - 1)
    def _():
        cn = _layer_norm(c_scr[...], clg_ref[0], clb_ref[0])
        ca = (_silu(cn) * _silu(hbuf[:, C_ZC:C_Q])).astype(BF16)
        yc = _dot(ca, wcp_ref[0])
        for h in range(N_HEADS):
            hs = slice(h * HEAD_DIM, (h + 1) * HEAD_DIM)
            o = o_scr[:, hs]
            ms = jnp.mean(o * o, axis=-1, keepdims=True)
            on = o * lax.rsqrt(ms + LN_EPS) * hgg_ref[0]
            o_scr[:, hs] = on * _silu(hbuf[:, C_ZH + h * HEAD_DIM:C_ZH + (h + 1) * HEAD_DIM])
        yh = _dot(o_scr[...].astype(BF16), whp_ref[0])
        merged = _sigmoid(hbuf[:, C_MC:C_MH]) * yc + _sigmoid(hbuf[:, C_MH:IN_COLS]) * yh
        out = _dot(merged.astype(BF16), wout_ref[0])
        x_new = _layer_norm(ALPHA * xcur[...] + out, lng_ref[0], lnb_ref[0])
        xcur[...] = x_new
        y_ref[...] = x_new


def _decode(x, state_conv, state_hgrn, p):
    n = x.shape[0]
    bb = DEC_BLOCK
    nb = n // bb
    vec = lambda m: _resident((1, 1, m), lambda l, j: (l, 0, 0))
    in_specs = [
        _resident((n, D_MODEL), lambda l, j: (0, 0)),
        pl.BlockSpec((1, bb, CONV_WIDTH - 1, D_MODEL), lambda l, j: (l, j, 0, 0)),
        pl.BlockSpec((1, bb, N_HEADS, HEAD_DIM, HEAD_DIM), lambda l, j: (l, j, 0, 0, 0)),
        _resident((1, D_MODEL, IN_COLS), lambda l, j: (l, 0, 0)),
        vec(IN_COLS),
        _resident((1, CONV_WIDTH, D_MODEL), lambda l, j: (l, 0, 0)),
        vec(D_MODEL), vec(D_MODEL), vec(D_MODEL),
        _resident((1, D_MODEL, D_MODEL), lambda l, j: (l, 0, 0)),
        _resident((DEPTH, D_MODEL), lambda l, j: (0, 0)),
        vec(HEAD_DIM),
        _resident((1, D_MODEL, D_MODEL), lambda l, j: (l, 0, 0)),
        _resident((1, D_MODEL, D_MODEL), lambda l, j: (l, 0, 0)),
        vec(D_MODEL), vec(D_MODEL),
    ]
    out_specs = [
        pl.BlockSpec((n, D_MODEL), lambda l, j: (0, 0)),
        pl.BlockSpec((1, bb, CONV_WIDTH - 1, D_MODEL), lambda l, j: (l, j, 0, 0)),
        pl.BlockSpec((1, bb, N_HEADS, HEAD_DIM, HEAD_DIM), lambda l, j: (l, j, 0, 0, 0)),
    ]
    out_shape = [
        jax.ShapeDtypeStruct((n, D_MODEL), F32),
        jax.ShapeDtypeStruct(state_conv.shape, F32),
        jax.ShapeDtypeStruct(state_hgrn.shape, F32),
    ]
    scratch = [
        pltpu.VMEM((n, D_MODEL), F32),
        pltpu.VMEM((n, IN_COLS), F32),
        pltpu.VMEM((8, D_MODEL), F32),
        pltpu.VMEM((D_MODEL, n), F32),
        pltpu.VMEM((D_MODEL, n), F32),
        pltpu.VMEM((D_MODEL, n), F32),
        pltpu.VMEM((n, D_MODEL), F32),
        pltpu.VMEM((n, D_MODEL), F32),
        pltpu.VMEM((n, D_MODEL), F32),
        pltpu.VMEM((bb, D_MODEL), F32),
        pltpu.VMEM((bb, D_MODEL), F32),
    ]
    kern = functools.partial(_decode_kernel, nb=nb, bb=bb)
    return pl.pallas_call(
        kern,
        grid=(DEPTH, nb),
        in_specs=in_specs,
        out_specs=out_specs,
        out_shape=out_shape,
        scratch_shapes=scratch,
        compiler_params=pltpu.CompilerParams(
            dimension_semantics=("arbitrary", "arbitrary"),
            vmem_limit_bytes=VMEM_LIMIT_BYTES),
        name="decode_layers",
    )(x, state_conv, state_hgrn, p["w_in"], p["b_in"], p["conv_w"], p["conv_b"], p["conv_ln_g"],
      p["conv_ln_b"], p["w_conv_proj"], p["lower_bounds"], p["hg_norm_g"], p["w_hg_proj"],
      p["w_out"], p["ln_g"], p["ln_b"])


def kernel(x_prompt, x_sample, state_conv, state_hgrn, w_in, b_in, conv_w, conv_b, conv_ln_g,
           conv_ln_b, w_conv_proj, lower_bounds, hg_norm_g, w_hg_proj, w_out, ln_g, ln_b):
    row = lambda a: a.reshape(DEPTH, 1, a.shape[-1])
    p = dict(
        w_in=w_in.astype(BF16), b_in=row(b_in), conv_w=conv_w, conv_b=row(conv_b),
        conv_ln_g=row(conv_ln_g), conv_ln_b=row(conv_ln_b), w_conv_proj=w_conv_proj.astype(BF16),
        lower_bounds=lower_bounds, hg_norm_g=row(hg_norm_g), w_hg_proj=w_hg_proj.astype(BF16),
        w_out=w_out.astype(BF16), ln_g=row(ln_g), ln_b=row(ln_b))

    xp = x_prompt
    conv_p, hgrn_p = [], []
    for layer in range(DEPTH):
        xp, cs, hs = _prompt_layer(layer, xp, p)
        conv_p.append(cs)
        hgrn_p.append(hs)

    n = x_sample.shape[0]
    ys, conv_s, hgrn_s = _decode(x_sample.reshape(n, D_MODEL), state_conv, state_hgrn, p)
    return (xp, ys.reshape(x_sample.shape), jnp.stack(conv_p), jnp.stack(hgrn_p), conv_s, hgrn_s)
```

```python
import functools

import jax
import jax.numpy as jnp
from jax import lax
from jax.experimental import pallas as pl
from jax.experimental.pallas import tpu as pltpu

F32 = jnp.float32
BF16 = jnp.bfloat16

D_MODEL = 1024
DEPTH = 4
CONV_WIDTH = 31
N_HEADS = 8
HEAD_DIM = 128
IN_COLS = 9 * D_MODEL
LN_EPS = 1e-5
G_FLOOR = 1e-30
ALPHA = (2 * DEPTH) ** 0.25

C_GLU_A, C_GLU_B, C_ZC, C_Q, C_F, C_I, C_ZH, C_MC, C_MH = (i * D_MODEL for i in range(9))

SUBLANES = 8
LANES = 128
HALO = 32
HALO_SKIP = HALO - (CONV_WIDTH - 1)
CHUNK = 128
PAIR = 2 * CHUNK
SUB = 32
N_SLABS = D_MODEL // LANES
TOK_BLOCK = 8
TAP_SLACK = 2
U_PITCH = HALO + CHUNK + 4
C_PITCH = CHUNK + 4
HG_CHUNK = 64
DEC_BLOCK = 8
VMEM_LIMIT_BYTES = 58 * 1024 * 1024


def _dot(a, b):
    return jnp.dot(a, b, preferred_element_type=F32)


def _dot_nt(a, b):
    return lax.dot_general(a, b, (((1,), (1,)), ((), ())), preferred_element_type=F32)


def _dot_tn(a, b):
    return lax.dot_general(a, b, (((0,), (0,)), ((), ())), preferred_element_type=F32)


def _pack_rows(w):
    *lead, k, n = w.shape
    pairs = w.astype(BF16).reshape(*lead, k // 2, 2, n)
    return lax.bitcast_convert_type(jnp.swapaxes(pairs, -1, -2), jnp.uint32)


def _unpack(w):
    return pltpu.bitcast(w, BF16)


def _sigmoid(x):
    return 0.5 * jnp.tanh(0.5 * x) + 0.5


def _silu(x):
    h = 0.5 * x
    return h + h * jnp.tanh(h)


def _layer_norm(x, g, b):
    mu = jnp.mean(x, axis=-1, keepdims=True)
    d = x - mu
    var = jnp.mean(d * d, axis=-1, keepdims=True)
    return d * lax.rsqrt(var + LN_EPS) * g + b


def _lower_bounds(lb_all):
    m = jnp.max(lb_all, axis=0, keepdims=True)
    e = jnp.exp(lb_all - m)
    p = e / jnp.sum(e, axis=0, keepdims=True)
    out, run = [], jnp.zeros_like(p[0:1])
    for l in range(DEPTH):
        run = run + p[l:l + 1]
        out.append(run - p[0:1])
    return out


def _rows(i, n):
    return pl.ds(pl.multiple_of(i * n, n), n)


def _tie(x, tok):
    reps = (x.shape[0] // SUBLANES, x.shape[1] // LANES)
    return pltpu.bitcast(pltpu.bitcast(x, jnp.uint32) | jnp.tile(tok, reps), x.dtype)


def _slab(k, pitch, start, n):
    return slice(k * pitch + start, k * pitch + start + n)


def _conv_tokens(u2, cw_ref, cb_ref, c2, t0, tok):
    win = [u2[pl.ds(t0 + HALO_SKIP + i, N_SLABS, stride=U_PITCH), :]
           for i in range(TOK_BLOCK + CONV_WIDTH - 1)]
    accs = [_tie(cb_ref[0], tok)] * TOK_BLOCK
    gates = [tok] * TAP_SLACK
    for w in range(CONV_WIDTH):
        cw = _tie(cw_ref[0, w], gates[w % TAP_SLACK])
        accs = [acc + win[i + w] * cw for i, acc in enumerate(accs)]
        gates[w % TAP_SLACK] = pltpu.bitcast(accs[0], jnp.uint32) & tok
    for i, acc in enumerate(accs):
        c2[pl.ds(t0 + i, N_SLABS, stride=C_PITCH), :] = acc
    return accs[-1]


def _hgrn_chunk(h_r, r, c0, c1, hgg, tri3, causal, st_t, acth_w, reset, tok):
    mid = HG_CHUNK // 2 - 1
    qs = _silu(_tie(h_r[r, C_Q:C_F], tok))
    pt = c1 * jnp.tanh(0.5 * _tie(h_r[r, C_F:C_I], tok))
    k = c1 - pt
    lg = jnp.log(jnp.maximum(c0 + pt, G_FLOOR))
    hi = lg.astype(BF16)
    r1 = lg - hi.astype(F32)
    md = r1.astype(BF16)
    lo = (r1 - md.astype(F32)).astype(BF16)
    gc = _dot(tri3, jnp.concatenate([hi, md, lo], axis=0))
    g_mid = gc[mid:mid + 1, :]
    g_last = gc[HG_CHUNK - 1:HG_CHUNK, :]
    qt = qs * jnp.exp(gc - g_mid)
    kt = k * jnp.exp(g_mid - gc)
    qh = (qt * jnp.exp(g_mid)).astype(BF16)
    kh = (kt * jnp.exp(g_last - g_mid)).astype(BF16)
    qt = qt.astype(BF16)
    kt = kt.astype(BF16)
    dec = jnp.exp(g_last)
    v = h_r[r, C_I:C_ZH]
    vb = v.astype(BF16)
    for h in range(N_HEADS):
        sl = slice(h * HEAD_DIM, (h + 1) * HEAD_DIM)
        a = jnp.where(causal, _dot_nt(qt[:, sl], kt[:, sl]), 0.0)
        s_t = st_t[h]
        if reset is not None:
            s_t = jnp.where(reset, 0.0, s_t)
        o = _dot(a.astype(BF16), vb[:, sl]) + _dot_nt(qh[:, sl], s_t.astype(BF16))
        v_t = v[:, sl].T.astype(BF16)
        st_t[h] = s_t * dec[:, sl] + _dot(v_t, kh[:, sl])
        ms = jnp.mean(o * o, axis=-1, keepdims=True)
        on = o * lax.rsqrt(ms + LN_EPS) * hgg
        z = h_r[r, C_ZH + h * HEAD_DIM:C_ZH + (h + 1) * HEAD_DIM]
        acth_w[r, sl] = (on * _silu(z)).astype(BF16)


def _prompt_layer_kernel(fence_ref, xn_ref, xp_ref, win_ref, bin_ref, cw_ref, cb_ref, clg_ref, clb_ref, wcp_ref,
                         lb_ref, hgg_ref, whp_ref, wout_ref, lng_ref, lnb_ref,
                         y_ref, convst_ref, st_ref,
                         h0, h1, actc0, actc1, acth0, acth1, u2, c2, g2, yc2, yh2,
                         m2, o2, st_t, xb_scr, xpb_scr, snap_u, snap_st, *, layer, pairs_per_row):
    s = pl.program_id(0)
    zero_u32 = fence_ref[0].astype(jnp.uint32)

    @pl.when(s == 0)
    def _():
        for ref in (h0, h1, actc0, actc1, acth0, acth1, u2, st_t):
            ref[...] = jnp.zeros_like(ref)

    lb = _lower_bounds(lb_ref[...])[layer]
    c0 = 0.5 + 0.5 * lb
    c1 = 0.5 - 0.5 * lb
    hgg = hgg_ref[0]
    ri = lax.broadcasted_iota(jnp.int32, (HG_CHUNK, HG_CHUNK), 0)
    ci = lax.broadcasted_iota(jnp.int32, (HG_CHUNK, HG_CHUNK), 1)
    causal = ri >= ci
    ri3 = lax.broadcasted_iota(jnp.int32, (HG_CHUNK, 3 * HG_CHUNK), 0)
    ci3 = lax.broadcasted_iota(jnp.int32, (HG_CHUNK, 3 * HG_CHUNK), 1)
    tri3 = (ri3 >= (ci3 & (HG_CHUNK - 1))).astype(BF16)
    n_sub = CHUNK // SUB

    def tick(parity, row_start):
        h_w, h_r = (h0, h1) if parity == 0 else (h1, h0)
        g_scr, yc_scr, yh_scr, m_scr, o_scr = (ref.at[parity] for ref in (g2, yc2, yh2, m2, o2))
        actc_w, acth_w = (actc1, acth1) if parity == 0 else (actc0, acth0)
        actc_r, acth_r = (actc0, acth0) if parity == 0 else (actc1, acth1)
        rows = slice(parity * CHUNK, (parity + 1) * CHUNK)

        sub = lambda i: slice(i * SUB, (i + 1) * SUB)
        col = lambda g: slice(g * D_MODEL, (g + 1) * D_MODEL)

        def token(res):
            return pltpu.bitcast(res[0:SUBLANES, 0:LANES], jnp.uint32) & zero_u32

        def in_proj(g):
            res = _dot(xb_scr[...], _unpack(win_ref[0, :, col(g)])) + bin_ref[0, :, col(g)]
            h_w[:, col(g)] = res
            return token(res)

        def glu(i, tok):
            b = _tie(h_r[sub(i), C_GLU_B:C_ZC], tok)
            u = h_r[sub(i), C_GLU_A:C_GLU_B] * _sigmoid(b)
            for k in range(N_SLABS):
                u2[_slab(k, U_PITCH, HALO + i * SUB, SUB), :] = u[:, k * LANES:(k + 1) * LANES]

        def conv(j, tok):
            for t0 in range(2 * j * TOK_BLOCK, 2 * (j + 1) * TOK_BLOCK, TOK_BLOCK):
                last = _conv_tokens(u2, cw_ref, cb_ref, c2, t0, tok | conv_chain[0])
                conv_chain[0] = pltpu.bitcast(last, jnp.uint32) & zero_u32

        def conv_ln(i, tok):
            c = jnp.concatenate([c2[_slab(k, C_PITCH, i * SUB, SUB), :] for k in range(N_SLABS)], axis=1)
            cn = _layer_norm(_tie(c, tok), clg_ref[0], clb_ref[0])
            actc_w[sub(i), :] = (_silu(cn) * _silu(h_r[sub(i), C_ZC:C_Q])).astype(BF16)

        def hgrn(c, tok):
            r = slice(c * HG_CHUNK, (c + 1) * HG_CHUNK)
            _hgrn_chunk(h_r, r, c0, c1, hgg, tri3, causal, st_t, acth_w,
                        row_start if c == 0 else None, tok)

        def gate_proj(g):
            cg = slice(C_MC + g * D_MODEL, C_MC + (g + 1) * D_MODEL)
            res = _dot(xpb_scr[...], _unpack(win_ref[0, :, cg])) + bin_ref[0, :, cg]
            g_scr[:, col(g)] = res
            return token(res)

        def branch_proj(act_ref, w_ref, y_scr):
            res = _dot(act_ref[...], _unpack(w_ref[0]))
            y_scr[...] = res
            return token(res)

        def merge(i):
            m_scr[sub(i), :] = (_sigmoid(g_scr[sub(i), col(0)]) * yc_scr[sub(i), :]
                                + _sigmoid(g_scr[sub(i), col(1)]) * yh_scr[sub(i), :]).astype(BF16)

        def post_ln(i):
            rr = slice(parity * CHUNK + i * SUB, parity * CHUNK + (i + 1) * SUB)
            res = ALPHA * xp_ref[0, rr, :] + o_scr[sub(i), :]
            y_ref[0, rr, :] = _layer_norm(res, lng_ref[0], lnb_ref[0])

        xb_scr[...] = xn_ref[0, rows, :].astype(BF16)
        xpb_scr[...] = xp_ref[0, rows, :].astype(BF16)
        conv_chain = [jnp.zeros((SUBLANES, LANES), jnp.uint32)]
        if row_start is not None:
            for k in range(N_SLABS):
                hist = _slab(k, U_PITCH, 0, HALO)
                u2[hist, :] = jnp.where(row_start, 0.0, u2[hist, :])
        tok = in_proj(0); glu(0, tok); glu(1, tok)
        tok = in_proj(1); glu(2, tok); glu(3, tok)
        conv(0, in_proj(2))
        conv(1, in_proj(3))
        conv(2, in_proj(4))
        conv(3, in_proj(5))
        conv(4, in_proj(6))
        conv(5, gate_proj(0))
        conv(6, gate_proj(1))
        conv(7, branch_proj(actc_r, wcp_ref, yc_scr))
        for k in range(N_SLABS):
            u2[_slab(k, U_PITCH, 0, HALO), :] = u2[_slab(k, U_PITCH, CHUNK, HALO), :]
        tok = branch_proj(acth_r, whp_ref, yh_scr)
        for i in range(n_sub):
            conv_ln(i, tok)
        hgrn(0, tok)
        for i in range(n_sub):
            merge(i)
        o_scr[...] = _dot(m_scr[...], _unpack(wout_ref[0]))
        hgrn(1, tok)
        for i in range(n_sub):
            post_ln(i)

    tick(0, None)
    for k in range(N_SLABS):
        snap_u[k * HALO:(k + 1) * HALO, :] = u2[_slab(k, U_PITCH, 0, HALO), :]
    snap_st[...] = st_t[...]

    tick(1, s % pairs_per_row == 0)

    @pl.when((s > 0) & (s % pairs_per_row == 0))
    def _():
        for k in range(N_SLABS):
            convst_ref[0, :, k * LANES:(k + 1) * LANES] = snap_u[k * HALO + HALO_SKIP:(k + 1) * HALO, :]
        for h in range(N_HEADS):
            st_ref[0, h] = snap_st[h].T


def _resident(shape, index_map):
    return pl.BlockSpec(shape, index_map, pipeline_mode=pl.Buffered(1))


def _prompt_layer(layer, x, p):
    B, T, _ = x.shape
    ppr = T // PAIR
    n_pairs = B * ppr
    l = layer

    def next_pair(s):
        q = jnp.minimum(s, n_pairs - 1)
        return (q // ppr, q % ppr, 0)

    def prev_pair(s):
        q = jnp.maximum(s - 1, 0)
        return (q // ppr, q % ppr, 0)

    def state_row(s):
        return jnp.clip((2 * s - 1) // (2 * ppr), 0, B - 1)

    vec = lambda n: _resident((1, 1, n), lambda s: (l, 0, 0))
    in_specs = [
        pl.BlockSpec(memory_space=pltpu.SMEM),
        pl.BlockSpec((1, PAIR, D_MODEL), next_pair),
        pl.BlockSpec((1, PAIR, D_MODEL), prev_pair),
        _resident((1, D_MODEL // 2, IN_COLS), lambda s: (l, 0, 0)),
        vec(IN_COLS),
        _resident((1, CONV_WIDTH, N_SLABS, LANES), lambda s: (l, 0, 0, 0)),
        _resident((1, N_SLABS, LANES), lambda s: (l, 0, 0)),
        vec(D_MODEL), vec(D_MODEL),
        _resident((1, D_MODEL // 2, D_MODEL), lambda s: (l, 0, 0)),
        _resident((DEPTH, D_MODEL), lambda s: (0, 0)),
        vec(HEAD_DIM),
        _resident((1, D_MODEL // 2, D_MODEL), lambda s: (l, 0, 0)),
        _resident((1, D_MODEL // 2, D_MODEL), lambda s: (l, 0, 0)),
        vec(D_MODEL), vec(D_MODEL),
    ]
    out_specs = [
        pl.BlockSpec((1, PAIR, D_MODEL), prev_pair),
        pl.BlockSpec((1, CONV_WIDTH - 1, D_MODEL), lambda s: (state_row(s), 0, 0)),
        pl.BlockSpec((1, N_HEADS, HEAD_DIM, HEAD_DIM), lambda s: (state_row(s), 0, 0, 0)),
    ]
    out_shape = [
        jax.ShapeDtypeStruct((B, T, D_MODEL), F32),
        jax.ShapeDtypeStruct((B, CONV_WIDTH - 1, D_MODEL), F32),
        jax.ShapeDtypeStruct((B, N_HEADS, HEAD_DIM, HEAD_DIM), F32),
    ]
    scratch = [
        pltpu.VMEM((CHUNK, C_MC), F32), pltpu.VMEM((CHUNK, C_MC), F32),
        pltpu.VMEM((CHUNK, D_MODEL), BF16), pltpu.VMEM((CHUNK, D_MODEL), BF16),
        pltpu.VMEM((CHUNK, D_MODEL), BF16), pltpu.VMEM((CHUNK, D_MODEL), BF16),
        pltpu.VMEM((N_SLABS * U_PITCH, LANES), F32),
        pltpu.VMEM((N_SLABS * C_PITCH, LANES), F32),
        pltpu.VMEM((2, CHUNK, 2 * D_MODEL), F32),
        pltpu.VMEM((2, CHUNK, D_MODEL), F32), pltpu.VMEM((2, CHUNK, D_MODEL), F32),
        pltpu.VMEM((2, CHUNK, D_MODEL), BF16),
        pltpu.VMEM((2, CHUNK, D_MODEL), F32),
        pltpu.VMEM((N_HEADS, HEAD_DIM, HEAD_DIM), F32),
        pltpu.VMEM((CHUNK, D_MODEL), BF16), pltpu.VMEM((CHUNK, D_MODEL), BF16),
        pltpu.VMEM((N_SLABS * HALO, LANES), F32),
        pltpu.VMEM((N_HEADS, HEAD_DIM, HEAD_DIM), F32),
    ]
    kern = functools.partial(_prompt_layer_kernel, layer=layer, pairs_per_row=ppr)
    return pl.pallas_call(
        kern,
        grid=(n_pairs + 1,),
        in_specs=in_specs,
        out_specs=out_specs,
        out_shape=out_shape,
        scratch_shapes=scratch,
        compiler_params=pltpu.CompilerParams(
            dimension_semantics=("arbitrary",),
            vmem_limit_bytes=VMEM_LIMIT_BYTES),
        name=f"prompt_layer_{layer}",
    )(jnp.zeros((1,), jnp.int32), x, x, p["w_in"], p["b_in"], p["conv_w_tok"], p["conv_b_tok"],
      p["conv_ln_g"], p["conv_ln_b"],
      p["w_conv_proj"], p["lower_bounds"], p["hg_norm_g"], p["w_hg_proj"], p["w_out"],
      p["ln_g"], p["ln_b"])


def _decode_kernel(x_ref, sc_ref, sh_ref, win_ref, bin_ref, cw_ref, cb_ref, clg_ref, clb_ref,
                   wcp_ref, lb_ref, hgg_ref, whp_ref, wout_ref, lng_ref, lnb_ref,
                   y_ref, sco_ref, sho_ref,
                   xcur, hbuf, lbs_scr, q_t, g_t, k_t, u_scr, o_scr, c_scr, o_stage, c_stage,
                   *, nb, bb):
    l = pl.program_id(0)
    j = pl.program_id(1)
    n = xcur.shape[0]

    @pl.when((l == 0) & (j == 0))
    def _():
        xcur[...] = x_ref[...]
        lbs = _lower_bounds(lb_ref[...])
        lbs_scr[...] = jnp.zeros_like(lbs_scr)
        for i in range(DEPTH):
            lbs_scr[i:i + 1, :] = lbs[i]

    @pl.when(j == 0)
    def _():
        hbuf[...] = _dot(xcur[...].astype(BF16), _unpack(win_ref[0])) + bin_ref[0]
        u_scr[...] = hbuf[:, C_GLU_A:C_GLU_B] * _sigmoid(hbuf[:, C_GLU_B:C_ZC])
        layer_row = lax.broadcasted_iota(jnp.int32, lbs_scr.shape, 0) == l
        lb = jnp.sum(jnp.where(layer_row, lbs_scr[...], 0.0), axis=0, keepdims=True)
        g = lb + (1.0 - lb) * _sigmoid(hbuf[:, C_F:C_I])
        q_t[...] = _silu(hbuf[:, C_Q:C_F]).T
        g_t[...] = jnp.maximum(g, G_FLOOR).T
        k_t[...] = (1.0 - g).T

    shift = (n - j * bb) % n
    qc = pltpu.roll(q_t[...], shift, 1)
    gc = pltpu.roll(g_t[...], shift, 1)
    kc = pltpu.roll(k_t[...], shift, 1)
    cw_hist = cw_ref[0, 0:CONV_WIDTH - 1, :]
    cw_new = cw_ref[0, CONV_WIDTH - 1:CONV_WIDTH, :]
    rows = _rows(j, bb)
    u_blk = u_scr[rows, :]
    v_blk = hbuf[rows, C_I:C_ZH]
    for b in range(bb):
        u_row = u_blk[b:b + 1, :]
        hist = sc_ref[0, b]
        c_stage[b:b + 1, :] = (jnp.sum(hist * cw_hist, axis=0, keepdims=True)
                               + u_row * cw_new + cb_ref[0])
        sco_ref[0, b, 0:CONV_WIDTH - 2, :] = sc_ref[0, b, 1:CONV_WIDTH - 1, :]
        sco_ref[0, b, CONV_WIDTH - 2:CONV_WIDTH - 1, :] = u_row
        for h in range(N_HEADS):
            hs = slice(h * HEAD_DIM, (h + 1) * HEAD_DIM)
            s_new = gc[hs, b:b + 1] * sh_ref[0, b, h] + kc[hs, b:b + 1] * v_blk[b:b + 1, hs]
            sho_ref[0, b, h] = s_new
            o_stage[b:b + 1, hs] = jnp.sum(qc[hs, b:b + 1] * s_new, axis=0, keepdims=True)
    c_scr[rows, :] = c_stage[...]
    o_scr[rows, :] = o_stage[...]

    @pl.when(j == nb - 1)
    def _():
        cn = _layer_norm(c_scr[...], clg_ref[0], clb_ref[0])
        ca = (_silu(cn) * _silu(hbuf[:, C_ZC:C_Q])).astype(BF16)
        yc = _dot(ca, _unpack(wcp_ref[0]))
        for h in range(N_HEADS):
            hs = slice(h * HEAD_DIM, (h + 1) * HEAD_DIM)
            o = o_scr[:, hs]
            ms = jnp.mean(o * o, axis=-1, keepdims=True)
            on = o * lax.rsqrt(ms + LN_EPS) * hgg_ref[0]
            o_scr[:, hs] = on * _silu(hbuf[:, C_ZH + h * HEAD_DIM:C_ZH + (h + 1) * HEAD_DIM])
        yh = _dot(o_scr[...].astype(BF16), _unpack(whp_ref[0]))
        merged = _sigmoid(hbuf[:, C_MC:C_MH]) * yc + _sigmoid(hbuf[:, C_MH:IN_COLS]) * yh
        out = _dot(merged.astype(BF16), _unpack(wout_ref[0]))
        x_new = _layer_norm(ALPHA * xcur[...] + out, lng_ref[0], lnb_ref[0])
        xcur[...] = x_new
        y_ref[...] = x_new


def _decode(x, state_conv, state_hgrn, p):
    n = x.shape[0]
    bb = DEC_BLOCK
    nb = n // bb
    vec = lambda m: _resident((1, 1, m), lambda l, j: (l, 0, 0))
    in_specs = [
        _resident((n, D_MODEL), lambda l, j: (0, 0)),
        pl.BlockSpec((1, bb, CONV_WIDTH - 1, D_MODEL), lambda l, j: (l, j, 0, 0)),
        pl.BlockSpec((1, bb, N_HEADS, HEAD_DIM, HEAD_DIM), lambda l, j: (l, j, 0, 0, 0)),
        _resident((1, D_MODEL // 2, IN_COLS), lambda l, j: (l, 0, 0)),
        vec(IN_COLS),
        _resident((1, CONV_WIDTH, D_MODEL), lambda l, j: (l, 0, 0)),
        vec(D_MODEL), vec(D_MODEL), vec(D_MODEL),
        _resident((1, D_MODEL // 2, D_MODEL), lambda l, j: (l, 0, 0)),
        _resident((DEPTH, D_MODEL), lambda l, j: (0, 0)),
        vec(HEAD_DIM),
        _resident((1, D_MODEL // 2, D_MODEL), lambda l, j: (l, 0, 0)),
        _resident((1, D_MODEL // 2, D_MODEL), lambda l, j: (l, 0, 0)),
        vec(D_MODEL), vec(D_MODEL),
    ]
    out_specs = [
        pl.BlockSpec((n, D_MODEL), lambda l, j: (0, 0)),
        pl.BlockSpec((1, bb, CONV_WIDTH - 1, D_MODEL), lambda l, j: (l, j, 0, 0)),
        pl.BlockSpec((1, bb, N_HEADS, HEAD_DIM, HEAD_DIM), lambda l, j: (l, j, 0, 0, 0)),
    ]
    out_shape = [
        jax.ShapeDtypeStruct((n, D_MODEL), F32),
        jax.ShapeDtypeStruct(state_conv.shape, F32),
        jax.ShapeDtypeStruct(state_hgrn.shape, F32),
    ]
    scratch = [
        pltpu.VMEM((n, D_MODEL), F32),
        pltpu.VMEM((n, IN_COLS), F32),
        pltpu.VMEM((8, D_MODEL), F32),
        pltpu.VMEM((D_MODEL, n), F32),
        pltpu.VMEM((D_MODEL, n), F32),
        pltpu.VMEM((D_MODEL, n), F32),
        pltpu.VMEM((n, D_MODEL), F32),
        pltpu.VMEM((n, D_MODEL), F32),
        pltpu.VMEM((n, D_MODEL), F32),
        pltpu.VMEM((bb, D_MODEL), F32),
        pltpu.VMEM((bb, D_MODEL), F32),
    ]
    kern = functools.partial(_decode_kernel, nb=nb, bb=bb)
    return pl.pallas_call(
        kern,
        grid=(DEPTH, nb),
        in_specs=in_specs,
        out_specs=out_specs,
        out_shape=out_shape,
        scratch_shapes=scratch,
        compiler_params=pltpu.CompilerParams(
            dimension_semantics=("arbitrary", "arbitrary"),
            vmem_limit_bytes=VMEM_LIMIT_BYTES),
        name="decode_layers",
    )(x, state_conv, state_hgrn, p["w_in"], p["b_in"], p["conv_w"], p["conv_b"], p["conv_ln_g"],
      p["conv_ln_b"], p["w_conv_proj"], p["lower_bounds"], p["hg_norm_g"], p["w_hg_proj"],
      p["w_out"], p["ln_g"], p["ln_b"])


def kernel(x_prompt, x_sample, state_conv, state_hgrn, w_in, b_in, conv_w, conv_b, conv_ln_g,
           conv_ln_b, w_conv_proj, lower_bounds, hg_norm_g, w_hg_proj, w_out, ln_g, ln_b):
    row = lambda a: a.reshape(DEPTH, 1, a.shape[-1])
    p = dict(
        w_in=_pack_rows(w_in), b_in=row(b_in), conv_w=conv_w, conv_b=row(conv_b),
        conv_ln_g=row(conv_ln_g), conv_ln_b=row(conv_ln_b), w_conv_proj=_pack_rows(w_conv_proj),
        lower_bounds=lower_bounds, hg_norm_g=row(hg_norm_g), w_hg_proj=_pack_rows(w_hg_proj),
        w_out=_pack_rows(w_out), ln_g=row(ln_g), ln_b=row(ln_b),
        conv_w_tok=conv_w.reshape(DEPTH, CONV_WIDTH, N_SLABS, LANES),
        conv_b_tok=conv_b.reshape(DEPTH, N_SLABS, LANES))

    xp = x_prompt
    conv_p, hgrn_p = [], []
    for layer in range(DEPTH):
        xp, cs, hs = _prompt_layer(layer, xp, p)
        conv_p.append(cs)
        hgrn_p.append(hs)

    n = x_sample.shape[0]
    ys, conv_s, hgrn_s = _decode(x_sample.reshape(n, D_MODEL), state_conv, state_hgrn, p)
    return (xp, ys.reshape(x_sample.shape), jnp.stack(conv_p), jnp.stack(hgrn_p), conv_s, hgrn_s)
```

```python
import functools

import jax
import jax.numpy as jnp
from jax import lax
from jax.experimental import pallas as pl
from jax.experimental.pallas import tpu as pltpu

F32 = jnp.float32
BF16 = jnp.bfloat16

D_MODEL = 1024
DEPTH = 4
CONV_WIDTH = 31
N_HEADS = 8
HEAD_DIM = 128
IN_COLS = 9 * D_MODEL
LN_EPS = 1e-5
G_FLOOR = 1e-30
ALPHA = (2 * DEPTH) ** 0.25

C_GLU_A, C_GLU_B, C_ZC, C_Q, C_F, C_I, C_ZH, C_MC, C_MH = (i * D_MODEL for i in range(9))

SUBLANES = 8
LANES = 128
HALO = 32
HALO_SKIP = HALO - (CONV_WIDTH - 1)
CHUNK = 128
PAIR = 2 * CHUNK
SUB = 32
N_SLABS = D_MODEL // LANES
TOK_BLOCK = 8
TAP_SLACK = 2
U_PITCH = HALO + CHUNK + 4
C_PITCH = CHUNK + 4
HG_CHUNK = 64
DEC_BLOCK = 8
PACK_COLS = 1024
VMEM_LIMIT_BYTES = 58 * 1024 * 1024


def _dot(a, b):
    return jnp.dot(a, b, preferred_element_type=F32)


def _dot_nt(a, b):
    return lax.dot_general(a, b, (((1,), (1,)), ((), ())), preferred_element_type=F32)


def _pack_kernel(w_ref, o_ref):
    o_ref[0] = pltpu.bitcast(w_ref[0].astype(BF16), jnp.uint32)


def _pack_rows(w):
    n_layers, k, n = w.shape
    bn = min(n, PACK_COLS)
    return pl.pallas_call(
        _pack_kernel,
        grid=(n_layers, n // bn),
        in_specs=[pl.BlockSpec((1, k, bn), lambda l, j: (l, 0, j))],
        out_specs=pl.BlockSpec((1, k // 2, bn), lambda l, j: (l, 0, j)),
        out_shape=jax.ShapeDtypeStruct((n_layers, k // 2, n), jnp.uint32),
        compiler_params=pltpu.CompilerParams(dimension_semantics=("arbitrary", "arbitrary")),
        name="pack_weights",
    )(w)


def _unpack(w):
    return pltpu.bitcast(w, BF16)


def _sigmoid(x):
    return 0.5 * jnp.tanh(0.5 * x) + 0.5


def _silu(x):
    h = 0.5 * x
    return h + h * jnp.tanh(h)


def _layer_norm(x, g, b):
    mu = jnp.mean(x, axis=-1, keepdims=True)
    d = x - mu
    var = jnp.mean(d * d, axis=-1, keepdims=True)
    return d * lax.rsqrt(var + LN_EPS) * g + b


def _lower_bounds(lb_all):
    m = jnp.max(lb_all, axis=0, keepdims=True)
    e = jnp.exp(lb_all - m)
    p = e / jnp.sum(e, axis=0, keepdims=True)
    out, run = [], jnp.zeros_like(p[0:1])
    for l in range(DEPTH):
        run = run + p[l:l + 1]
        out.append(run - p[0:1])
    return out


def _rows(i, n):
    return pl.ds(pl.multiple_of(i * n, n), n)


def _tie(x, tok):
    reps = (x.shape[0] // SUBLANES, x.shape[1] // LANES)
    return pltpu.bitcast(pltpu.bitcast(x, jnp.uint32) | jnp.tile(tok, reps), x.dtype)


def _slab(k, pitch, start, n):
    return slice(k * pitch + start, k * pitch + start + n)


def _conv_tokens(u2, cw_ref, cb_ref, c2, t0, tok):
    win = [u2[pl.ds(t0 + HALO_SKIP + i, N_SLABS, stride=U_PITCH), :]
           for i in range(TOK_BLOCK + CONV_WIDTH - 1)]
    accs = [_tie(cb_ref[0], tok)] * TOK_BLOCK
    gates = [tok] * TAP_SLACK
    for w in range(CONV_WIDTH):
        cw = _tie(cw_ref[0, w], gates[w % TAP_SLACK])
        accs = [acc + win[i + w] * cw for i, acc in enumerate(accs)]
        gates[w % TAP_SLACK] = pltpu.bitcast(accs[0], jnp.uint32) & tok
    for i, acc in enumerate(accs):
        c2[pl.ds(t0 + i, N_SLABS, stride=C_PITCH), :] = acc
    return accs[-1]


def _hgrn_chunk(h_r, r, c0, c1, hgg, tri3, causal, st_t, acth_w, reset, tok):
    mid = HG_CHUNK // 2 - 1
    qs = _silu(_tie(h_r[r, C_Q:C_F], tok))
    pt = c1 * jnp.tanh(0.5 * _tie(h_r[r, C_F:C_I], tok))
    k = c1 - pt
    lg = jnp.log(jnp.maximum(c0 + pt, G_FLOOR))
    hi = lg.astype(BF16)
    r1 = lg - hi.astype(F32)
    md = r1.astype(BF16)
    lo = (r1 - md.astype(F32)).astype(BF16)
    gc = _dot(tri3, jnp.concatenate([hi, md, lo], axis=0))
    g_mid = gc[mid:mid + 1, :]
    g_last = gc[HG_CHUNK - 1:HG_CHUNK, :]
    qt = qs * jnp.exp(gc - g_mid)
    kt = k * jnp.exp(g_mid - gc)
    qh = (qt * jnp.exp(g_mid)).astype(BF16)
    kh = (kt * jnp.exp(g_last - g_mid)).astype(BF16)
    qt = qt.astype(BF16)
    kt = kt.astype(BF16)
    dec = jnp.exp(g_last)
    v = h_r[r, C_I:C_ZH]
    vb = v.astype(BF16)
    for h in range(N_HEADS):
        sl = slice(h * HEAD_DIM, (h + 1) * HEAD_DIM)
        a = jnp.where(causal, _dot_nt(qt[:, sl], kt[:, sl]), 0.0)
        s_t = st_t[h]
        if reset is not None:
            s_t = jnp.where(reset, 0.0, s_t)
        o = _dot(a.astype(BF16), vb[:, sl]) + _dot_nt(qh[:, sl], s_t.astype(BF16))
        v_t = v[:, sl].T.astype(BF16)
        st_t[h] = s_t * dec[:, sl] + _dot(v_t, kh[:, sl])
        ms = jnp.mean(o * o, axis=-1, keepdims=True)
        on = o * lax.rsqrt(ms + LN_EPS) * hgg
        z = h_r[r, C_ZH + h * HEAD_DIM:C_ZH + (h + 1) * HEAD_DIM]
        acth_w[r, sl] = (on * _silu(z)).astype(BF16)


def _prompt_layer_kernel(fence_ref, xn_ref, xp_ref, win_ref, bin_ref, cw_ref, cb_ref, clg_ref, clb_ref, wcp_ref,
                         lb_ref, hgg_ref, whp_ref, wout_ref, lng_ref, lnb_ref,
                         y_ref, convst_ref, st_ref,
                         h0, h1, actc0, actc1, acth0, acth1, u2, c2, g2, yc2, yh2,
                         m2, o2, st_t, xb_scr, xpb_scr, snap_u, snap_st, *, layer, pairs_per_row):
    s = pl.program_id(0)
    zero_u32 = fence_ref[0].astype(jnp.uint32)

    @pl.when(s == 0)
    def _():
        for ref in (h0, h1, actc0, actc1, acth0, acth1, u2, st_t):
            ref[...] = jnp.zeros_like(ref)

    lb = _lower_bounds(lb_ref[...])[layer]
    c0 = 0.5 + 0.5 * lb
    c1 = 0.5 - 0.5 * lb
    hgg = hgg_ref[0]
    ri = lax.broadcasted_iota(jnp.int32, (HG_CHUNK, HG_CHUNK), 0)
    ci = lax.broadcasted_iota(jnp.int32, (HG_CHUNK, HG_CHUNK), 1)
    causal = ri >= ci
    ri3 = lax.broadcasted_iota(jnp.int32, (HG_CHUNK, 3 * HG_CHUNK), 0)
    ci3 = lax.broadcasted_iota(jnp.int32, (HG_CHUNK, 3 * HG_CHUNK), 1)
    tri3 = (ri3 >= (ci3 & (HG_CHUNK - 1))).astype(BF16)
    n_sub = CHUNK // SUB

    def tick(parity, row_start):
        h_w, h_r = (h0, h1) if parity == 0 else (h1, h0)
        g_scr, yc_scr, yh_scr, m_scr, o_scr = (ref.at[parity] for ref in (g2, yc2, yh2, m2, o2))
        actc_w, acth_w = (actc1, acth1) if parity == 0 else (actc0, acth0)
        actc_r, acth_r = (actc0, acth0) if parity == 0 else (actc1, acth1)
        rows = slice(parity * CHUNK, (parity + 1) * CHUNK)

        sub = lambda i: slice(i * SUB, (i + 1) * SUB)
        col = lambda g: slice(g * D_MODEL, (g + 1) * D_MODEL)

        def token(res):
            return pltpu.bitcast(res[0:SUBLANES, 0:LANES], jnp.uint32) & zero_u32

        def in_proj(g):
            res = _dot(xb_scr[...], _unpack(win_ref[0, :, col(g)])) + bin_ref[0, :, col(g)]
            h_w[:, col(g)] = res
            return token(res)

        def glu(i, tok):
            b = _tie(h_r[sub(i), C_GLU_B:C_ZC], tok)
            u = h_r[sub(i), C_GLU_A:C_GLU_B] * _sigmoid(b)
            for k in range(N_SLABS):
                u2[_slab(k, U_PITCH, HALO + i * SUB, SUB), :] = u[:, k * LANES:(k + 1) * LANES]

        def conv(j, tok):
            for t0 in range(2 * j * TOK_BLOCK, 2 * (j + 1) * TOK_BLOCK, TOK_BLOCK):
                last = _conv_tokens(u2, cw_ref, cb_ref, c2, t0, tok | conv_chain[0])
                conv_chain[0] = pltpu.bitcast(last, jnp.uint32) & zero_u32

        def conv_ln(i, tok):
            c = jnp.concatenate([c2[_slab(k, C_PITCH, i * SUB, SUB), :] for k in range(N_SLABS)], axis=1)
            cn = _layer_norm(_tie(c, tok), clg_ref[0], clb_ref[0])
            actc_w[sub(i), :] = (_silu(cn) * _silu(h_r[sub(i), C_ZC:C_Q])).astype(BF16)

        def hgrn(c, tok):
            r = slice(c * HG_CHUNK, (c + 1) * HG_CHUNK)
            _hgrn_chunk(h_r, r, c0, c1, hgg, tri3, causal, st_t, acth_w,
                        row_start if c == 0 else None, tok)

        def gate_proj(g):
            cg = slice(C_MC + g * D_MODEL, C_MC + (g + 1) * D_MODEL)
            res = _dot(xpb_scr[...], _unpack(win_ref[0, :, cg])) + bin_ref[0, :, cg]
            g_scr[:, col(g)] = res
            return token(res)

        def branch_proj(act_ref, w_ref, y_scr):
            res = _dot(act_ref[...], _unpack(w_ref[0]))
            y_scr[...] = res
            return token(res)

        def merge(i):
            m_scr[sub(i), :] = (_sigmoid(g_scr[sub(i), col(0)]) * yc_scr[sub(i), :]
                                + _sigmoid(g_scr[sub(i), col(1)]) * yh_scr[sub(i), :]).astype(BF16)

        def post_ln(i):
            rr = slice(parity * CHUNK + i * SUB, parity * CHUNK + (i + 1) * SUB)
            res = ALPHA * xp_ref[0, rr, :] + o_scr[sub(i), :]
            y_ref[0, rr, :] = _layer_norm(res, lng_ref[0], lnb_ref[0])

        conv_chain = [jnp.zeros((SUBLANES, LANES), jnp.uint32)]

        def load_x():
            xb_scr[...] = xn_ref[0, rows, :].astype(BF16)
            xpb_scr[...] = xp_ref[0, rows, :].astype(BF16)

        def reset_history():
            if row_start is not None:
                for k in range(N_SLABS):
                    hist = _slab(k, U_PITCH, 0, HALO)
                    u2[hist, :] = jnp.where(row_start, 0.0, u2[hist, :])

        def keep_history():
            for k in range(N_SLABS):
                u2[_slab(k, U_PITCH, 0, HALO), :] = u2[_slab(k, U_PITCH, CHUNK, HALO), :]

        def glus(a, b, tok):
            for i in range(a, b):
                glu(i, tok)

        def conv_lns(tok):
            for i in range(n_sub):
                conv_ln(i, tok)

        def merge_out():
            for i in range(n_sub):
                merge(i)
            o_scr[...] = _dot(m_scr[...], _unpack(wout_ref[0]))

        def post_lns():
            for i in range(n_sub):
                post_ln(i)

        return dict(load_x=load_x, reset_history=reset_history, keep_history=keep_history,
                    in_proj=in_proj, gate_proj=gate_proj, glus=glus, conv=conv, conv_lns=conv_lns,
                    hgrn=hgrn, merge_out=merge_out, post_lns=post_lns,
                    proj_c=functools.partial(branch_proj, actc_r, wcp_ref, yc_scr),
                    proj_h=functools.partial(branch_proj, acth_r, whp_ref, yh_scr))

    def run_tick(t):
        t["load_x"]()
        t["reset_history"]()
        tok = t["in_proj"](0); t["glus"](0, 2, tok)
        tok = t["in_proj"](1); t["glus"](2, 4, tok)
        for j in range(5):
            t["conv"](j, t["in_proj"](2 + j))
        t["conv"](5, t["gate_proj"](0))
        t["conv"](6, t["gate_proj"](1))
        t["conv"](7, t["proj_c"]())
        t["keep_history"]()
        tok = t["proj_h"]()
        t["conv_lns"](tok)
        t["hgrn"](0, tok)
        t["merge_out"]()
        t["hgrn"](1, tok)
        t["post_lns"]()

    run_tick(tick(0, None))
    for k in range(N_SLABS):
        snap_u[k * HALO:(k + 1) * HALO, :] = u2[_slab(k, U_PITCH, 0, HALO), :]
    snap_st[...] = st_t[...]
    run_tick(tick(1, s % pairs_per_row == 0))

    @pl.when((s > 0) & (s % pairs_per_row == 0))
    def _():
        for k in range(N_SLABS):
            convst_ref[0, :, k * LANES:(k + 1) * LANES] = snap_u[k * HALO + HALO_SKIP:(k + 1) * HALO, :]
        for h in range(N_HEADS):
            st_ref[0, h] = snap_st[h].T


def _resident(shape, index_map):
    return pl.BlockSpec(shape, index_map, pipeline_mode=pl.Buffered(1))


def _prompt_layer(layer, x, p):
    B, T, _ = x.shape
    ppr = T // PAIR
    n_pairs = B * ppr
    l = layer

    def next_pair(s):
        q = jnp.minimum(s, n_pairs - 1)
        return (q // ppr, q % ppr, 0)

    def prev_pair(s):
        q = jnp.maximum(s - 1, 0)
        return (q // ppr, q % ppr, 0)

    def state_row(s):
        return jnp.clip((2 * s - 1) // (2 * ppr), 0, B - 1)

    vec = lambda n: _resident((1, 1, n), lambda s: (l, 0, 0))
    in_specs = [
        pl.BlockSpec(memory_space=pltpu.SMEM),
        pl.BlockSpec((1, PAIR, D_MODEL), next_pair),
        pl.BlockSpec((1, PAIR, D_MODEL), prev_pair),
        _resident((1, D_MODEL // 2, IN_COLS), lambda s: (l, 0, 0)),
        vec(IN_COLS),
        _resident((1, CONV_WIDTH, N_SLABS, LANES), lambda s: (l, 0, 0, 0)),
        _resident((1, N_SLABS, LANES), lambda s: (l, 0, 0)),
        vec(D_MODEL), vec(D_MODEL),
        _resident((1, D_MODEL // 2, D_MODEL), lambda s: (l, 0, 0)),
        _resident((DEPTH, D_MODEL), lambda s: (0, 0)),
        vec(HEAD_DIM),
        _resident((1, D_MODEL // 2, D_MODEL), lambda s: (l, 0, 0)),
        _resident((1, D_MODEL // 2, D_MODEL), lambda s: (l, 0, 0)),
        vec(D_MODEL), vec(D_MODEL),
    ]
    out_specs = [
        pl.BlockSpec((1, PAIR, D_MODEL), prev_pair),
        pl.BlockSpec((1, CONV_WIDTH - 1, D_MODEL), lambda s: (state_row(s), 0, 0)),
        pl.BlockSpec((1, N_HEADS, HEAD_DIM, HEAD_DIM), lambda s: (state_row(s), 0, 0, 0)),
    ]
    out_shape = [
        jax.ShapeDtypeStruct((B, T, D_MODEL), F32),
        jax.ShapeDtypeStruct((B, CONV_WIDTH - 1, D_MODEL), F32),
        jax.ShapeDtypeStruct((B, N_HEADS, HEAD_DIM, HEAD_DIM), F32),
    ]
    scratch = [
        pltpu.VMEM((CHUNK, C_MC), F32), pltpu.VMEM((CHUNK, C_MC), F32),
        pltpu.VMEM((CHUNK, D_MODEL), BF16), pltpu.VMEM((CHUNK, D_MODEL), BF16),
        pltpu.VMEM((CHUNK, D_MODEL), BF16), pltpu.VMEM((CHUNK, D_MODEL), BF16),
        pltpu.VMEM((N_SLABS * U_PITCH, LANES), F32),
        pltpu.VMEM((N_SLABS * C_PITCH, LANES), F32),
        pltpu.VMEM((2, CHUNK, 2 * D_MODEL), F32),
        pltpu.VMEM((2, CHUNK, D_MODEL), F32), pltpu.VMEM((2, CHUNK, D_MODEL), F32),
        pltpu.VMEM((2, CHUNK, D_MODEL), BF16),
        pltpu.VMEM((2, CHUNK, D_MODEL), F32),
        pltpu.VMEM((N_HEADS, HEAD_DIM, HEAD_DIM), F32),
        pltpu.VMEM((CHUNK, D_MODEL), BF16), pltpu.VMEM((CHUNK, D_MODEL), BF16),
        pltpu.VMEM((N_SLABS * HALO, LANES), F32),
        pltpu.VMEM((N_HEADS, HEAD_DIM, HEAD_DIM), F32),
    ]
    kern = functools.partial(_prompt_layer_kernel, layer=layer, pairs_per_row=ppr)
    return pl.pallas_call(
        kern,
        grid=(n_pairs + 1,),
        in_specs=in_specs,
        out_specs=out_specs,
        out_shape=out_shape,
        scratch_shapes=scratch,
        compiler_params=pltpu.CompilerParams(
            dimension_semantics=("arbitrary",),
            vmem_limit_bytes=VMEM_LIMIT_BYTES),
        name=f"prompt_layer_{layer}",
    )(jnp.zeros((1,), jnp.int32), x, x, p["w_in"], p["b_in"], p["conv_w_tok"], p["conv_b_tok"],
      p["conv_ln_g"], p["conv_ln_b"],
      p["w_conv_proj"], p["lower_bounds"], p["hg_norm_g"], p["w_hg_proj"], p["w_out"],
      p["ln_g"], p["ln_b"])


def _decode_kernel(x_ref, sc_ref, sh_ref, win_ref, bin_ref, cw_ref, cb_ref, clg_ref, clb_ref,
                   wcp_ref, lb_ref, hgg_ref, whp_ref, wout_ref, lng_ref, lnb_ref,
                   y_ref, sco_ref, sho_ref,
                   xcur, hbuf, lbs_scr, q_t, g_t, k_t, u_scr, o_scr, c_scr, o_stage, c_stage,
                   *, nb, bb):
    l = pl.program_id(0)
    j = pl.program_id(1)
    n = xcur.shape[0]

    @pl.when((l == 0) & (j == 0))
    def _():
        xcur[...] = x_ref[...]
        lbs = _lower_bounds(lb_ref[...])
        lbs_scr[...] = jnp.zeros_like(lbs_scr)
        for i in range(DEPTH):
            lbs_scr[i:i + 1, :] = lbs[i]

    @pl.when(j == 0)
    def _():
        hbuf[...] = _dot(xcur[...].astype(BF16), _unpack(win_ref[0])) + bin_ref[0]
        u_scr[...] = hbuf[:, C_GLU_A:C_GLU_B] * _sigmoid(hbuf[:, C_GLU_B:C_ZC])
        layer_row = lax.broadcasted_iota(jnp.int32, lbs_scr.shape, 0) == l
        lb = jnp.sum(jnp.where(layer_row, lbs_scr[...], 0.0), axis=0, keepdims=True)
        g = lb + (1.0 - lb) * _sigmoid(hbuf[:, C_F:C_I])
        q_t[...] = _silu(hbuf[:, C_Q:C_F]).T
        g_t[...] = jnp.maximum(g, G_FLOOR).T
        k_t[...] = (1.0 - g).T

    shift = (n - j * bb) % n
    qc = pltpu.roll(q_t[...], shift, 1)
    gc = pltpu.roll(g_t[...], shift, 1)
    kc = pltpu.roll(k_t[...], shift, 1)
    cw_hist = cw_ref[0, 0:CONV_WIDTH - 1, :]
    cw_new = cw_ref[0, CONV_WIDTH - 1:CONV_WIDTH, :]
    rows = _rows(j, bb)
    u_blk = u_scr[rows, :]
    v_blk = hbuf[rows, C_I:C_ZH]
    for b in range(bb):
        u_row = u_blk[b:b + 1, :]
        hist = sc_ref[0, b]
        c_stage[b:b + 1, :] = (jnp.sum(hist * cw_hist, axis=0, keepdims=True)
                               + u_row * cw_new + cb_ref[0])
        sco_ref[0, b, 0:CONV_WIDTH - 2, :] = sc_ref[0, b, 1:CONV_WIDTH - 1, :]
        sco_ref[0, b, CONV_WIDTH - 2:CONV_WIDTH - 1, :] = u_row
        for h in range(N_HEADS):
            hs = slice(h * HEAD_DIM, (h + 1) * HEAD_DIM)
            s_new = gc[hs, b:b + 1] * sh_ref[0, b, h] + kc[hs, b:b + 1] * v_blk[b:b + 1, hs]
            sho_ref[0, b, h] = s_new
            o_stage[b:b + 1, hs] = jnp.sum(qc[hs, b:b + 1] * s_new, axis=0, keepdims=True)
    c_scr[rows, :] = c_stage[...]
    o_scr[rows, :] = o_stage[...]

    @pl.when(j == nb - 1)
    def _():
        cn = _layer_norm(c_scr[...], clg_ref[0], clb_ref[0])
        ca = (_silu(cn) * _silu(hbuf[:, C_ZC:C_Q])).astype(BF16)
        yc = _dot(ca, _unpack(wcp_ref[0]))
        for h in range(N_HEADS):
            hs = slice(h * HEAD_DIM, (h + 1) * HEAD_DIM)
            o = o_scr[:, hs]
            ms = jnp.mean(o * o, axis=-1, keepdims=True)
            on = o * lax.rsqrt(ms + LN_EPS) * hgg_ref[0]
            o_scr[:, hs] = on * _silu(hbuf[:, C_ZH + h * HEAD_DIM:C_ZH + (h + 1) * HEAD_DIM])
        yh = _dot(o_scr[...].astype(BF16), _unpack(whp_ref[0]))
        merged = _sigmoid(hbuf[:, C_MC:C_MH]) * yc + _sigmoid(hbuf[:, C_MH:IN_COLS]) * yh
        out = _dot(merged.astype(BF16), _unpack(wout_ref[0]))
        x_new = _layer_norm(ALPHA * xcur[...] + out, lng_ref[0], lnb_ref[0])
        xcur[...] = x_new
        y_ref[...] = x_new


def _decode(x, state_conv, state_hgrn, p):
    n = x.shape[0]
    bb = DEC_BLOCK
    nb = n // bb
    vec = lambda m: _resident((1, 1, m), lambda l, j: (l, 0, 0))
    in_specs = [
        _resident((n, D_MODEL), lambda l, j: (0, 0)),
        pl.BlockSpec((1, bb, CONV_WIDTH - 1, D_MODEL), lambda l, j: (l, j, 0, 0)),
        pl.BlockSpec((1, bb, N_HEADS, HEAD_DIM, HEAD_DIM), lambda l, j: (l, j, 0, 0, 0)),
        _resident((1, D_MODEL // 2, IN_COLS), lambda l, j: (l, 0, 0)),
        vec(IN_COLS),
        _resident((1, CONV_WIDTH, D_MODEL), lambda l, j: (l, 0, 0)),
        vec(D_MODEL), vec(D_MODEL), vec(D_MODEL),
        _resident((1, D_MODEL // 2, D_MODEL), lambda l, j: (l, 0, 0)),
        _resident((DEPTH, D_MODEL), lambda l, j: (0, 0)),
        vec(HEAD_DIM),
        _resident((1, D_MODEL // 2, D_MODEL), lambda l, j: (l, 0, 0)),
        _resident((1, D_MODEL // 2, D_MODEL), lambda l, j: (l, 0, 0)),
        vec(D_MODEL), vec(D_MODEL),
    ]
    out_specs = [
        pl.BlockSpec((n, D_MODEL), lambda l, j: (0, 0)),
        pl.BlockSpec((1, bb, CONV_WIDTH - 1, D_MODEL), lambda l, j: (l, j, 0, 0)),
        pl.BlockSpec((1, bb, N_HEADS, HEAD_DIM, HEAD_DIM), lambda l, j: (l, j, 0, 0, 0)),
    ]
    out_shape = [
        jax.ShapeDtypeStruct((n, D_MODEL), F32),
        jax.ShapeDtypeStruct(state_conv.shape, F32),
        jax.ShapeDtypeStruct(state_hgrn.shape, F32),
    ]
    scratch = [
        pltpu.VMEM((n, D_MODEL), F32),
        pltpu.VMEM((n, IN_COLS), F32),
        pltpu.VMEM((8, D_MODEL), F32),
        pltpu.VMEM((D_MODEL, n), F32),
        pltpu.VMEM((D_MODEL, n), F32),
        pltpu.VMEM((D_MODEL, n), F32),
        pltpu.VMEM((n, D_MODEL), F32),
        pltpu.VMEM((n, D_MODEL), F32),
        pltpu.VMEM((n, D_MODEL), F32),
        pltpu.VMEM((bb, D_MODEL), F32),
        pltpu.VMEM((bb, D_MODEL), F32),
    ]
    kern = functools.partial(_decode_kernel, nb=nb, bb=bb)
    return pl.pallas_call(
        kern,
        grid=(DEPTH, nb),
        in_specs=in_specs,
        out_specs=out_specs,
        out_shape=out_shape,
        scratch_shapes=scratch,
        compiler_params=pltpu.CompilerParams(
            dimension_semantics=("arbitrary", "arbitrary"),
            vmem_limit_bytes=VMEM_LIMIT_BYTES),
        name="decode_layers",
    )(x, state_conv, state_hgrn, p["w_in"], p["b_in"], p["conv_w"], p["conv_b"], p["conv_ln_g"],
      p["conv_ln_b"], p["w_conv_proj"], p["lower_bounds"], p["hg_norm_g"], p["w_hg_proj"],
      p["w_out"], p["ln_g"], p["ln_b"])


def kernel(x_prompt, x_sample, state_conv, state_hgrn, w_in, b_in, conv_w, conv_b, conv_ln_g,
           conv_ln_b, w_conv_proj, lower_bounds, hg_norm_g, w_hg_proj, w_out, ln_g, ln_b):
    row = lambda a: a.reshape(DEPTH, 1, a.shape[-1])
    p = dict(
        w_in=_pack_rows(w_in), b_in=row(b_in), conv_w=conv_w, conv_b=row(conv_b),
        conv_ln_g=row(conv_ln_g), conv_ln_b=row(conv_ln_b), w_conv_proj=_pack_rows(w_conv_proj),
        lower_bounds=lower_bounds, hg_norm_g=row(hg_norm_g), w_hg_proj=_pack_rows(w_hg_proj),
        w_out=_pack_rows(w_out), ln_g=row(ln_g), ln_b=row(ln_b),
        conv_w_tok=conv_w.reshape(DEPTH, CONV_WIDTH, N_SLABS, LANES),
        conv_b_tok=conv_b.reshape(DEPTH, N_SLABS, LANES))

    xp = x_prompt
    conv_p, hgrn_p = [], []
    for layer in range(DEPTH):
        xp, cs, hs = _prompt_layer(layer, xp, p)
        conv_p.append(cs)
        hgrn_p.append(hs)

    n = x_sample.shape[0]
    ys, conv_s, hgrn_s = _decode(x_sample.reshape(n, D_MODEL), state_conv, state_hgrn, p)
    return (xp, ys.reshape(x_sample.shape), jnp.stack(conv_p), jnp.stack(hgrn_p), conv_s, hgrn_s)
```

```python
import functools

import jax
import jax.numpy as jnp
from jax import lax
from jax.experimental import pallas as pl
from jax.experimental.pallas import tpu as pltpu

F32 = jnp.float32
BF16 = jnp.bfloat16

D_MODEL = 1024
DEPTH = 4
CONV_WIDTH = 31
N_HEADS = 8
HEAD_DIM = 128
IN_COLS = 9 * D_MODEL
LN_EPS = 1e-5
G_FLOOR = 1e-30
ALPHA = (2 * DEPTH) ** 0.25

C_GLU_A, C_GLU_B, C_ZC, C_Q, C_F, C_I, C_ZH, C_MC, C_MH = (i * D_MODEL for i in range(9))

SUBLANES = 8
LANES = 128
HALO = 32
HALO_SKIP = HALO - (CONV_WIDTH - 1)
CHUNK = 128
PAIR = 2 * CHUNK
SUB = 32
N_SLABS = D_MODEL // LANES
TOK_BLOCK = 8
TAP_SLACK = 2
U_PITCH = HALO + CHUNK + 4
C_PITCH = CHUNK + 4
HG_CHUNK = 64
DEC_BLOCK = 8
PACK_COLS = 1024
VMEM_LIMIT_BYTES = 58 * 1024 * 1024


def _dot(a, b):
    return jnp.dot(a, b, preferred_element_type=F32)


def _dot_nt(a, b):
    return lax.dot_general(a, b, (((1,), (1,)), ((), ())), preferred_element_type=F32)


def _pack_kernel(w_ref, o_ref):
    o_ref[0] = pltpu.bitcast(w_ref[0].astype(BF16), jnp.uint32)


def _pack_rows(w):
    n_layers, k, n = w.shape
    bn = min(n, PACK_COLS)
    return pl.pallas_call(
        _pack_kernel,
        grid=(n_layers, n // bn),
        in_specs=[pl.BlockSpec((1, k, bn), lambda l, j: (l, 0, j))],
        out_specs=pl.BlockSpec((1, k // 2, bn), lambda l, j: (l, 0, j)),
        out_shape=jax.ShapeDtypeStruct((n_layers, k // 2, n), jnp.uint32),
        compiler_params=pltpu.CompilerParams(dimension_semantics=("arbitrary", "arbitrary")),
        name="pack_weights",
    )(w)


def _unpack(w):
    return pltpu.bitcast(w, BF16)


def _sigmoid(x):
    return 0.5 * jnp.tanh(0.5 * x) + 0.5


def _silu(x):
    h = 0.5 * x
    return h + h * jnp.tanh(h)


def _layer_norm(x, g, b):
    mu = jnp.mean(x, axis=-1, keepdims=True)
    d = x - mu
    var = jnp.mean(d * d, axis=-1, keepdims=True)
    return d * lax.rsqrt(var + LN_EPS) * g + b


def _lower_bounds(lb_all):
    m = jnp.max(lb_all, axis=0, keepdims=True)
    e = jnp.exp(lb_all - m)
    p = e / jnp.sum(e, axis=0, keepdims=True)
    out, run = [], jnp.zeros_like(p[0:1])
    for l in range(DEPTH):
        run = run + p[l:l + 1]
        out.append(run - p[0:1])
    return out


def _rows(i, n):
    return pl.ds(pl.multiple_of(i * n, n), n)


def _tie(x, tok):
    reps = (x.shape[0] // SUBLANES, x.shape[1] // LANES)
    return pltpu.bitcast(pltpu.bitcast(x, jnp.uint32) | jnp.tile(tok, reps), x.dtype)


def _slab(k, pitch, start, n):
    return slice(k * pitch + start, k * pitch + start + n)


def _conv_tokens(u2, cw_ref, cb_ref, c2, t0, tok):
    win = [u2[pl.ds(t0 + HALO_SKIP + i, N_SLABS, stride=U_PITCH), :]
           for i in range(TOK_BLOCK + CONV_WIDTH - 1)]
    accs = [_tie(cb_ref[0], tok)] * TOK_BLOCK
    gates = [tok] * TAP_SLACK
    for w in range(CONV_WIDTH):
        cw = _tie(cw_ref[0, w], gates[w % TAP_SLACK])
        accs = [acc + win[i + w] * cw for i, acc in enumerate(accs)]
        gates[w % TAP_SLACK] = pltpu.bitcast(accs[0], jnp.uint32) & tok
    for i, acc in enumerate(accs):
        c2[pl.ds(t0 + i, N_SLABS, stride=C_PITCH), :] = acc
    return accs[-1]


def _hgrn_chunk(h_r, r, c0, c1, hgg, tri3, causal, st_t, acth_w, reset, tok):
    mid = HG_CHUNK // 2 - 1
    qs = _silu(_tie(h_r[r, C_Q:C_F], tok))
    pt = c1 * jnp.tanh(0.5 * _tie(h_r[r, C_F:C_I], tok))
    k = c1 - pt
    lg = jnp.log(jnp.maximum(c0 + pt, G_FLOOR))
    hi = lg.astype(BF16)
    r1 = lg - hi.astype(F32)
    md = r1.astype(BF16)
    lo = (r1 - md.astype(F32)).astype(BF16)
    gc = _dot(tri3, jnp.concatenate([hi, md, lo], axis=0))
    g_mid = gc[mid:mid + 1, :]
    g_last = gc[HG_CHUNK - 1:HG_CHUNK, :]
    qt = qs * jnp.exp(gc - g_mid)
    kt = k * jnp.exp(g_mid - gc)
    qh = (qt * jnp.exp(g_mid)).astype(BF16)
    kh = (kt * jnp.exp(g_last - g_mid)).astype(BF16)
    qt = qt.astype(BF16)
    kt = kt.astype(BF16)
    dec = jnp.exp(g_last)
    v = h_r[r, C_I:C_ZH]
    vb = v.astype(BF16)
    for h in range(N_HEADS):
        sl = slice(h * HEAD_DIM, (h + 1) * HEAD_DIM)
        a = jnp.where(causal, _dot_nt(qt[:, sl], kt[:, sl]), 0.0)
        s_t = st_t[h]
        if reset is not None:
            s_t = jnp.where(reset, 0.0, s_t)
        o = _dot(a.astype(BF16), vb[:, sl]) + _dot_nt(qh[:, sl], s_t.astype(BF16))
        v_t = v[:, sl].T.astype(BF16)
        st_t[h] = s_t * dec[:, sl] + _dot(v_t, kh[:, sl])
        ms = jnp.mean(o * o, axis=-1, keepdims=True)
        on = o * lax.rsqrt(ms + LN_EPS) * hgg
        z = h_r[r, C_ZH + h * HEAD_DIM:C_ZH + (h + 1) * HEAD_DIM]
        acth_w[r, sl] = (on * _silu(z)).astype(BF16)


def _prompt_layer_kernel(fence_ref, xn_ref, xp_ref, win_ref, bin_ref, cw_ref, cb_ref, clg_ref, clb_ref, wcp_ref,
                         lb_ref, hgg_ref, whp_ref, wout_ref, lng_ref, lnb_ref,
                         y_ref, convst_ref, st_ref,
                         h0, h1, actc0, actc1, acth0, acth1, u2, c2, g2, yc2, yh2,
                         m2, o2, st_t, xb_scr, xpb_scr, snap_u, snap_st, *, layer, pairs_per_row):
    s = pl.program_id(0)
    zero_u32 = fence_ref[0].astype(jnp.uint32)

    @pl.when(s == 0)
    def _():
        for ref in (h0, h1, actc0, actc1, acth0, acth1, u2, st_t):
            ref[...] = jnp.zeros_like(ref)

    lb = _lower_bounds(lb_ref[...])[layer]
    c0 = 0.5 + 0.5 * lb
    c1 = 0.5 - 0.5 * lb
    hgg = hgg_ref[0]
    ri = lax.broadcasted_iota(jnp.int32, (HG_CHUNK, HG_CHUNK), 0)
    ci = lax.broadcasted_iota(jnp.int32, (HG_CHUNK, HG_CHUNK), 1)
    causal = ri >= ci
    ri3 = lax.broadcasted_iota(jnp.int32, (HG_CHUNK, 3 * HG_CHUNK), 0)
    ci3 = lax.broadcasted_iota(jnp.int32, (HG_CHUNK, 3 * HG_CHUNK), 1)
    tri3 = (ri3 >= (ci3 & (HG_CHUNK - 1))).astype(BF16)
    n_sub = CHUNK // SUB

    def tick(parity, row_start):
        h_w, h_r = (h0, h1) if parity == 0 else (h1, h0)
        g_scr, yc_scr, yh_scr, m_scr, o_scr = (ref.at[parity] for ref in (g2, yc2, yh2, m2, o2))
        actc_w, acth_w = (actc1, acth1) if parity == 0 else (actc0, acth0)
        actc_r, acth_r = (actc0, acth0) if parity == 0 else (actc1, acth1)
        rows = slice(parity * CHUNK, (parity + 1) * CHUNK)

        sub = lambda i: slice(i * SUB, (i + 1) * SUB)
        col = lambda g: slice(g * D_MODEL, (g + 1) * D_MODEL)

        def token(res):
            return pltpu.bitcast(res[0:SUBLANES, 0:LANES], jnp.uint32) & zero_u32

        def in_proj(g):
            res = _dot(xb_scr[...], _unpack(win_ref[0, :, col(g)])) + bin_ref[0, :, col(g)]
            h_w[:, col(g)] = res
            return token(res)

        def glu(i, tok):
            b = _tie(h_r[sub(i), C_GLU_B:C_ZC], tok)
            u = h_r[sub(i), C_GLU_A:C_GLU_B] * _sigmoid(b)
            for k in range(N_SLABS):
                u2[_slab(k, U_PITCH, HALO + i * SUB, SUB), :] = u[:, k * LANES:(k + 1) * LANES]

        def conv(j, tok):
            for t0 in range(2 * j * TOK_BLOCK, 2 * (j + 1) * TOK_BLOCK, TOK_BLOCK):
                last = _conv_tokens(u2, cw_ref, cb_ref, c2, t0, tok | conv_chain[0])
                conv_chain[0] = pltpu.bitcast(last, jnp.uint32) & zero_u32

        def conv_ln(i, tok):
            c = jnp.concatenate([c2[_slab(k, C_PITCH, i * SUB, SUB), :] for k in range(N_SLABS)], axis=1)
            cn = _layer_norm(_tie(c, tok), clg_ref[0], clb_ref[0])
            actc_w[sub(i), :] = (_silu(cn) * _silu(h_r[sub(i), C_ZC:C_Q])).astype(BF16)

        def hgrn(c, tok):
            r = slice(c * HG_CHUNK, (c + 1) * HG_CHUNK)
            _hgrn_chunk(h_r, r, c0, c1, hgg, tri3, causal, st_t, acth_w,
                        row_start if c == 0 else None, tok)

        def gate_proj(g):
            cg = slice(C_MC + g * D_MODEL, C_MC + (g + 1) * D_MODEL)
            res = _dot(xpb_scr[...], _unpack(win_ref[0, :, cg])) + bin_ref[0, :, cg]
            g_scr[:, col(g)] = res
            return token(res)

        def branch_proj(act_ref, w_ref, y_scr):
            res = _dot(act_ref[...], _unpack(w_ref[0]))
            y_scr[...] = res
            return token(res)

        def merge(i):
            m_scr[sub(i), :] = (_sigmoid(g_scr[sub(i), col(0)]) * yc_scr[sub(i), :]
                                + _sigmoid(g_scr[sub(i), col(1)]) * yh_scr[sub(i), :]).astype(BF16)

        def post_ln(i):
            rr = slice(parity * CHUNK + i * SUB, parity * CHUNK + (i + 1) * SUB)
            res = ALPHA * xp_ref[0, rr, :] + o_scr[sub(i), :]
            y_ref[0, rr, :] = _layer_norm(res, lng_ref[0], lnb_ref[0])

        conv_chain = [jnp.zeros((SUBLANES, LANES), jnp.uint32)]

        def load_x():
            xb_scr[...] = xn_ref[0, rows, :].astype(BF16)
            xpb_scr[...] = xp_ref[0, rows, :].astype(BF16)

        def reset_history():
            if row_start is not None:
                for k in range(N_SLABS):
                    hist = _slab(k, U_PITCH, 0, HALO)
                    u2[hist, :] = jnp.where(row_start, 0.0, u2[hist, :])

        def keep_history():
            for k in range(N_SLABS):
                u2[_slab(k, U_PITCH, 0, HALO), :] = u2[_slab(k, U_PITCH, CHUNK, HALO), :]

        def glus(a, b, tok):
            for i in range(a, b):
                glu(i, tok)

        def conv_lns(tok):
            for i in range(n_sub):
                conv_ln(i, tok)

        def merge_out():
            for i in range(n_sub):
                merge(i)
            o_scr[...] = _dot(m_scr[...], _unpack(wout_ref[0]))

        def post_lns():
            for i in range(n_sub):
                post_ln(i)

        return dict(load_x=load_x, reset_history=reset_history, keep_history=keep_history,
                    in_proj=in_proj, gate_proj=gate_proj, glus=glus, conv=conv, conv_lns=conv_lns,
                    hgrn=hgrn, merge_out=merge_out, post_lns=post_lns,
                    proj_c=functools.partial(branch_proj, actc_r, wcp_ref, yc_scr),
                    proj_h=functools.partial(branch_proj, acth_r, whp_ref, yh_scr))

    def run_tick(t):
        t["load_x"]()
        t["reset_history"]()
        tok = t["in_proj"](0); t["glus"](0, 2, tok)
        tok = t["in_proj"](1); t["glus"](2, 4, tok)
        for j in range(5):
            t["conv"](j, t["in_proj"](2 + j))
        t["conv"](5, t["gate_proj"](0))
        t["conv"](6, t["gate_proj"](1))
        t["conv"](7, t["proj_c"]())
        t["keep_history"]()
        tok = t["proj_h"]()
        t["conv_lns"](tok)
        t["hgrn"](0, tok)
        t["merge_out"]()
        t["hgrn"](1, tok)
        t["post_lns"]()

    run_tick(tick(0, None))
    for k in range(N_SLABS):
        snap_u[k * HALO:(k + 1) * HALO, :] = u2[_slab(k, U_PITCH, 0, HALO), :]
    snap_st[...] = st_t[...]
    run_tick(tick(1, s % pairs_per_row == 0))

    @pl.when((s > 0) & (s % pairs_per_row == 0))
    def _():
        for k in range(N_SLABS):
            convst_ref[0, :, k * LANES:(k + 1) * LANES] = snap_u[k * HALO + HALO_SKIP:(k + 1) * HALO, :]
        for h in range(N_HEADS):
            st_ref[0, h] = snap_st[h].T


def _resident(shape, index_map):
    return pl.BlockSpec(shape, index_map, pipeline_mode=pl.Buffered(1))


def _prompt_layer(layer, x, p):
    B, T, _ = x.shape
    ppr = T // PAIR
    n_pairs = B * ppr
    l = layer

    def next_pair(s):
        q = jnp.minimum(s, n_pairs - 1)
        return (q // ppr, q % ppr, 0)

    def prev_pair(s):
        q = jnp.maximum(s - 1, 0)
        return (q // ppr, q % ppr, 0)

    def state_row(s):
        return jnp.clip((2 * s - 1) // (2 * ppr), 0, B - 1)

    vec = lambda n: _resident((1, 1, n), lambda s: (l, 0, 0))
    in_specs = [
        pl.BlockSpec(memory_space=pltpu.SMEM),
        pl.BlockSpec((1, PAIR, D_MODEL), next_pair),
        pl.BlockSpec((1, PAIR, D_MODEL), prev_pair),
        _resident((1, D_MODEL // 2, IN_COLS), lambda s: (l, 0, 0)),
        vec(IN_COLS),
        _resident((1, CONV_WIDTH, N_SLABS, LANES), lambda s: (l, 0, 0, 0)),
        _resident((1, N_SLABS, LANES), lambda s: (l, 0, 0)),
        vec(D_MODEL), vec(D_MODEL),
        _resident((1, D_MODEL // 2, D_MODEL), lambda s: (l, 0, 0)),
        _resident((DEPTH, D_MODEL), lambda s: (0, 0)),
        vec(HEAD_DIM),
        _resident((1, D_MODEL // 2, D_MODEL), lambda s: (l, 0, 0)),
        _resident((1, D_MODEL // 2, D_MODEL), lambda s: (l, 0, 0)),
        vec(D_MODEL), vec(D_MODEL),
    ]
    out_specs = [
        pl.BlockSpec((1, PAIR, D_MODEL), prev_pair),
        pl.BlockSpec((1, CONV_WIDTH - 1, D_MODEL), lambda s: (state_row(s), 0, 0)),
        pl.BlockSpec((1, N_HEADS, HEAD_DIM, HEAD_DIM), lambda s: (state_row(s), 0, 0, 0)),
    ]
    out_shape = [
        jax.ShapeDtypeStruct((B, T, D_MODEL), F32),
        jax.ShapeDtypeStruct((B, CONV_WIDTH - 1, D_MODEL), F32),
        jax.ShapeDtypeStruct((B, N_HEADS, HEAD_DIM, HEAD_DIM), F32),
    ]
    scratch = [
        pltpu.VMEM((CHUNK, C_MC), F32), pltpu.VMEM((CHUNK, C_MC), F32),
        pltpu.VMEM((CHUNK, D_MODEL), BF16), pltpu.VMEM((CHUNK, D_MODEL), BF16),
        pltpu.VMEM((CHUNK, D_MODEL), BF16), pltpu.VMEM((CHUNK, D_MODEL), BF16),
        pltpu.VMEM((N_SLABS * U_PITCH, LANES), F32),
        pltpu.VMEM((N_SLABS * C_PITCH, LANES), F32),
        pltpu.VMEM((2, CHUNK, 2 * D_MODEL), F32),
        pltpu.VMEM((2, CHUNK, D_MODEL), F32), pltpu.VMEM((2, CHUNK, D_MODEL), F32),
        pltpu.VMEM((2, CHUNK, D_MODEL), BF16),
        pltpu.VMEM((2, CHUNK, D_MODEL), F32),
        pltpu.VMEM((N_HEADS, HEAD_DIM, HEAD_DIM), F32),
        pltpu.VMEM((CHUNK, D_MODEL), BF16), pltpu.VMEM((CHUNK, D_MODEL), BF16),
        pltpu.VMEM((N_SLABS * HALO, LANES), F32),
        pltpu.VMEM((N_HEADS, HEAD_DIM, HEAD_DIM), F32),
    ]
    kern = functools.partial(_prompt_layer_kernel, layer=layer, pairs_per_row=ppr)
    return pl.pallas_call(
        kern,
        grid=(n_pairs + 1,),
        in_specs=in_specs,
        out_specs=out_specs,
        out_shape=out_shape,
        scratch_shapes=scratch,
        compiler_params=pltpu.CompilerParams(
            dimension_semantics=("arbitrary",),
            vmem_limit_bytes=VMEM_LIMIT_BYTES),
        name=f"prompt_layer_{layer}",
    )(jnp.zeros((1,), jnp.int32), x, x, p["w_in"], p["b_in"], p["conv_w_tok"], p["conv_b_tok"],
      p["conv_ln_g"], p["conv_ln_b"],
      p["w_conv_proj"], p["lower_bounds"], p["hg_norm_g"], p["w_hg_proj"], p["w_out"],
      p["ln_g"], p["ln_b"])


def _decode_kernel(x_ref, sc_ref, sh_ref, win_ref, bin_ref, cw_ref, cb_ref, clg_ref, clb_ref,
                   wcp_ref, lb_ref, hgg_ref, whp_ref, wout_ref, lng_ref, lnb_ref,
                   y_ref, sco_ref, sho_ref,
                   xcur, hbuf, lbs_scr, q_scr, g_t, k_t, u_scr, o_scr, c_scr, o_stage, c_stage,
                   *, nb, bb):
    l = pl.program_id(0)
    j = pl.program_id(1)
    n = xcur.shape[0]

    @pl.when((l == 0) & (j == 0))
    def _():
        xcur[...] = x_ref[...]
        lbs = _lower_bounds(lb_ref[...])
        lbs_scr[...] = jnp.zeros_like(lbs_scr)
        for i in range(DEPTH):
            lbs_scr[i:i + 1, :] = lbs[i]

    @pl.when(j == 0)
    def _():
        hbuf[...] = _dot(xcur[...].astype(BF16), _unpack(win_ref[0])) + bin_ref[0]
        u_scr[...] = hbuf[:, C_GLU_A:C_GLU_B] * _sigmoid(hbuf[:, C_GLU_B:C_ZC])
        layer_row = lax.broadcasted_iota(jnp.int32, lbs_scr.shape, 0) == l
        lb = jnp.sum(jnp.where(layer_row, lbs_scr[...], 0.0), axis=0, keepdims=True)
        g = lb + (1.0 - lb) * _sigmoid(hbuf[:, C_F:C_I])
        q_scr[...] = _silu(hbuf[:, C_Q:C_F])
        g_t[...] = jnp.maximum(g, G_FLOOR).T
        k_t[...] = (1.0 - g).T

    shift = (n - j * bb) % n
    gc = pltpu.roll(g_t[...], shift, 1)
    kc = pltpu.roll(k_t[...], shift, 1)
    cw_hist = cw_ref[0, 0:CONV_WIDTH - 1, :]
    cw_new = cw_ref[0, CONV_WIDTH - 1:CONV_WIDTH, :]
    rows = _rows(j, bb)
    u_blk = u_scr[rows, :]
    v_blk = hbuf[rows, C_I:C_ZH]
    q_blk = q_scr[rows, :].astype(BF16)
    for b in range(bb):
        u_row = u_blk[b:b + 1, :]
        hist = sc_ref[0, b]
        c_stage[b:b + 1, :] = (jnp.sum(hist * cw_hist, axis=0, keepdims=True)
                               + u_row * cw_new + cb_ref[0])
        sco_ref[0, b, 0:CONV_WIDTH - 2, :] = sc_ref[0, b, 1:CONV_WIDTH - 1, :]
        sco_ref[0, b, CONV_WIDTH - 2:CONV_WIDTH - 1, :] = u_row
        for h in range(N_HEADS):
            hs = slice(h * HEAD_DIM, (h + 1) * HEAD_DIM)
            s_new = gc[hs, b:b + 1] * sh_ref[0, b, h] + kc[hs, b:b + 1] * v_blk[b:b + 1, hs]
            sho_ref[0, b, h] = s_new
            o_stage[b:b + 1, hs] = _dot(q_blk[:, hs], s_new.astype(BF16))[b:b + 1, :]
    c_scr[rows, :] = c_stage[...]
    o_scr[rows, :] = o_stage[...]

    @pl.when(j == nb - 1)
    def _():
        cn = _layer_norm(c_scr[...], clg_ref[0], clb_ref[0])
        ca = (_silu(cn) * _silu(hbuf[:, C_ZC:C_Q])).astype(BF16)
        yc = _dot(ca, _unpack(wcp_ref[0]))
        for h in range(N_HEADS):
            hs = slice(h * HEAD_DIM, (h + 1) * HEAD_DIM)
            o = o_scr[:, hs]
            ms = jnp.mean(o * o, axis=-1, keepdims=True)
            on = o * lax.rsqrt(ms + LN_EPS) * hgg_ref[0]
            o_scr[:, hs] = on * _silu(hbuf[:, C_ZH + h * HEAD_DIM:C_ZH + (h + 1) * HEAD_DIM])
        yh = _dot(o_scr[...].astype(BF16), _unpack(whp_ref[0]))
        merged = _sigmoid(hbuf[:, C_MC:C_MH]) * yc + _sigmoid(hbuf[:, C_MH:IN_COLS]) * yh
        out = _dot(merged.astype(BF16), _unpack(wout_ref[0]))
        x_new = _layer_norm(ALPHA * xcur[...] + out, lng_ref[0], lnb_ref[0])
        xcur[...] = x_new
        y_ref[...] = x_new


def _decode(x, state_conv, state_hgrn, p):
    n = x.shape[0]
    bb = DEC_BLOCK
    nb = n // bb
    vec = lambda m: _resident((1, 1, m), lambda l, j: (l, 0, 0))
    in_specs = [
        _resident((n, D_MODEL), lambda l, j: (0, 0)),
        pl.BlockSpec((1, bb, CONV_WIDTH - 1, D_MODEL), lambda l, j: (l, j, 0, 0)),
        pl.BlockSpec((1, bb, N_HEADS, HEAD_DIM, HEAD_DIM), lambda l, j: (l, j, 0, 0, 0)),
        _resident((1, D_MODEL // 2, IN_COLS), lambda l, j: (l, 0, 0)),
        vec(IN_COLS),
        _resident((1, CONV_WIDTH, D_MODEL), lambda l, j: (l, 0, 0)),
        vec(D_MODEL), vec(D_MODEL), vec(D_MODEL),
        _resident((1, D_MODEL // 2, D_MODEL), lambda l, j: (l, 0, 0)),
        _resident((DEPTH, D_MODEL), lambda l, j: (0, 0)),
        vec(HEAD_DIM),
        _resident((1, D_MODEL // 2, D_MODEL), lambda l, j: (l, 0, 0)),
        _resident((1, D_MODEL // 2, D_MODEL), lambda l, j: (l, 0, 0)),
        vec(D_MODEL), vec(D_MODEL),
    ]
    out_specs = [
        pl.BlockSpec((n, D_MODEL), lambda l, j: (0, 0)),
        pl.BlockSpec((1, bb, CONV_WIDTH - 1, D_MODEL), lambda l, j: (l, j, 0, 0)),
        pl.BlockSpec((1, bb, N_HEADS, HEAD_DIM, HEAD_DIM), lambda l, j: (l, j, 0, 0, 0)),
    ]
    out_shape = [
        jax.ShapeDtypeStruct((n, D_MODEL), F32),
        jax.ShapeDtypeStruct(state_conv.shape, F32),
        jax.ShapeDtypeStruct(state_hgrn.shape, F32),
    ]
    scratch = [
        pltpu.VMEM((n, D_MODEL), F32),
        pltpu.VMEM((n, IN_COLS), F32),
        pltpu.VMEM((8, D_MODEL), F32),
        pltpu.VMEM((n, D_MODEL), F32),
        pltpu.VMEM((D_MODEL, n), F32),
        pltpu.VMEM((D_MODEL, n), F32),
        pltpu.VMEM((n, D_MODEL), F32),
        pltpu.VMEM((n, D_MODEL), F32),
        pltpu.VMEM((n, D_MODEL), F32),
        pltpu.VMEM((bb, D_MODEL), F32),
        pltpu.VMEM((bb, D_MODEL), F32),
    ]
    kern = functools.partial(_decode_kernel, nb=nb, bb=bb)
    return pl.pallas_call(
        kern,
        grid=(DEPTH, nb),
        in_specs=in_specs,
        out_specs=out_specs,
        out_shape=out_shape,
        scratch_shapes=scratch,
        compiler_params=pltpu.CompilerParams(
            dimension_semantics=("arbitrary", "arbitrary"),
            vmem_limit_bytes=VMEM_LIMIT_BYTES),
        name="decode_layers",
    )(x, state_conv, state_hgrn, p["w_in"], p["b_in"], p["conv_w"], p["conv_b"], p["conv_ln_g"],
      p["conv_ln_b"], p["w_conv_proj"], p["lower_bounds"], p["hg_norm_g"], p["w_hg_proj"],
      p["w_out"], p["ln_g"], p["ln_b"])


def kernel(x_prompt, x_sample, state_conv, state_hgrn, w_in, b_in, conv_w, conv_b, conv_ln_g,
           conv_ln_b, w_conv_proj, lower_bounds, hg_norm_g, w_hg_proj, w_out, ln_g, ln_b):
    row = lambda a: a.reshape(DEPTH, 1, a.shape[-1])
    p = dict(
        w_in=_pack_rows(w_in), b_in=row(b_in), conv_w=conv_w, conv_b=row(conv_b),
        conv_ln_g=row(conv_ln_g), conv_ln_b=row(conv_ln_b), w_conv_proj=_pack_rows(w_conv_proj),
        lower_bounds=lower_bounds, hg_norm_g=row(hg_norm_g), w_hg_proj=_pack_rows(w_hg_proj),
        w_out=_pack_rows(w_out), ln_g=row(ln_g), ln_b=row(ln_b),
        conv_w_tok=conv_w.reshape(DEPTH, CONV_WIDTH, N_SLABS, LANES),
        conv_b_tok=conv_b.reshape(DEPTH, N_SLABS, LANES))

    xp = x_prompt
    conv_p, hgrn_p = [], []
    for layer in range(DEPTH):
        xp, cs, hs = _prompt_layer(layer, xp, p)
        conv_p.append(cs)
        hgrn_p.append(hs)

    n = x_sample.shape[0]
    ys, conv_s, hgrn_s = _decode(x_sample.reshape(n, D_MODEL), state_conv, state_hgrn, p)
    return (xp, ys.reshape(x_sample.shape), jnp.stack(conv_p), jnp.stack(hgrn_p), conv_s, hgrn_s)
```

```python
import functools

import jax
import jax.numpy as jnp
from jax import lax
from jax.experimental import pallas as pl
from jax.experimental.pallas import tpu as pltpu

F32 = jnp.float32
BF16 = jnp.bfloat16

D_MODEL = 1024
DEPTH = 4
CONV_WIDTH = 31
N_HEADS = 8
HEAD_DIM = 128
IN_COLS = 9 * D_MODEL
LN_EPS = 1e-5
G_FLOOR = 1e-30
ALPHA = (2 * DEPTH) ** 0.25

C_GLU_A, C_GLU_B, C_ZC, C_Q, C_F, C_I, C_ZH, C_MC, C_MH = (i * D_MODEL for i in range(9))

SUBLANES = 8
LANES = 128
HALO = 32
HALO_SKIP = HALO - (CONV_WIDTH - 1)
CHUNK = 128
PAIR = 2 * CHUNK
SUB = 32
N_SLABS = D_MODEL // LANES
TOK_BLOCK = 8
TAP_SLACK = 2
U_PITCH = HALO + CHUNK + 4
C_PITCH = CHUNK + 4
HG_CHUNK = 64
DEC_BLOCK = 8
PACK_COLS = 1024
VMEM_LIMIT_BYTES = 58 * 1024 * 1024


def _dot(a, b):
    return jnp.dot(a, b, preferred_element_type=F32)


def _dot_nt(a, b):
    return lax.dot_general(a, b, (((1,), (1,)), ((), ())), preferred_element_type=F32)


def _pack_kernel(w_ref, o_ref):
    o_ref[0] = pltpu.bitcast(w_ref[0].astype(BF16), jnp.uint32)


def _pack_rows(w):
    n_layers, k, n = w.shape
    bn = min(n, PACK_COLS)
    return pl.pallas_call(
        _pack_kernel,
        grid=(n_layers, n // bn),
        in_specs=[pl.BlockSpec((1, k, bn), lambda l, j: (l, 0, j))],
        out_specs=pl.BlockSpec((1, k // 2, bn), lambda l, j: (l, 0, j)),
        out_shape=jax.ShapeDtypeStruct((n_layers, k // 2, n), jnp.uint32),
        compiler_params=pltpu.CompilerParams(dimension_semantics=("arbitrary", "arbitrary")),
        name="pack_weights",
    )(w)


def _unpack(w):
    return pltpu.bitcast(w, BF16)


def _sigmoid(x):
    return 0.5 * jnp.tanh(0.5 * x) + 0.5


def _silu(x):
    h = 0.5 * x
    return h + h * jnp.tanh(h)


def _layer_norm(x, g, b):
    mu = jnp.mean(x, axis=-1, keepdims=True)
    d = x - mu
    var = jnp.mean(d * d, axis=-1, keepdims=True)
    return d * lax.rsqrt(var + LN_EPS) * g + b


def _lower_bounds(lb_all):
    m = jnp.max(lb_all, axis=0, keepdims=True)
    e = jnp.exp(lb_all - m)
    p = e / jnp.sum(e, axis=0, keepdims=True)
    out, run = [], jnp.zeros_like(p[0:1])
    for l in range(DEPTH):
        run = run + p[l:l + 1]
        out.append(run - p[0:1])
    return out


def _rows(i, n):
    return pl.ds(pl.multiple_of(i * n, n), n)


def _tie(x, tok):
    reps = (x.shape[0] // SUBLANES, x.shape[1] // LANES)
    return pltpu.bitcast(pltpu.bitcast(x, jnp.uint32) | jnp.tile(tok, reps), x.dtype)


def _slab(k, pitch, start, n):
    return slice(k * pitch + start, k * pitch + start + n)


def _conv_tokens(u2, cw_ref, cb_ref, c2, t0, tok):
    win = [u2[pl.ds(t0 + HALO_SKIP + i, N_SLABS, stride=U_PITCH), :]
           for i in range(TOK_BLOCK + CONV_WIDTH - 1)]
    accs = [_tie(cb_ref[0], tok)] * TOK_BLOCK
    gates = [tok] * TAP_SLACK
    for w in range(CONV_WIDTH):
        cw = _tie(cw_ref[0, w], gates[w % TAP_SLACK])
        accs = [acc + win[i + w] * cw for i, acc in enumerate(accs)]
        gates[w % TAP_SLACK] = pltpu.bitcast(accs[0], jnp.uint32) & tok
    for i, acc in enumerate(accs):
        c2[pl.ds(t0 + i, N_SLABS, stride=C_PITCH), :] = acc
    return accs[-1]


def _hgrn_chunk(h_r, r, c0, c1, hgg, tri3, causal, st_t, acth_w, reset, tok):
    mid = HG_CHUNK // 2 - 1
    qs = _silu(_tie(h_r[r, C_Q:C_F], tok))
    pt = c1 * jnp.tanh(0.5 * _tie(h_r[r, C_F:C_I], tok))
    k = c1 - pt
    lg = jnp.log(jnp.maximum(c0 + pt, G_FLOOR))
    hi = lg.astype(BF16)
    r1 = lg - hi.astype(F32)
    md = r1.astype(BF16)
    lo = (r1 - md.astype(F32)).astype(BF16)
    gc = _dot(tri3, jnp.concatenate([hi, md, lo], axis=0))
    g_mid = gc[mid:mid + 1, :]
    g_last = gc[HG_CHUNK - 1:HG_CHUNK, :]
    qt = qs * jnp.exp(gc - g_mid)
    kt = k * jnp.exp(g_mid - gc)
    qh = (qt * jnp.exp(g_mid)).astype(BF16)
    kh = (kt * jnp.exp(g_last - g_mid)).astype(BF16)
    qt = qt.astype(BF16)
    kt = kt.astype(BF16)
    dec = jnp.exp(g_last)
    v = h_r[r, C_I:C_ZH]
    vb = v.astype(BF16)
    for h in range(N_HEADS):
        sl = slice(h * HEAD_DIM, (h + 1) * HEAD_DIM)
        a = jnp.where(causal, _dot_nt(qt[:, sl], kt[:, sl]), 0.0)
        s_t = st_t[h]
        if reset is not None:
            s_t = jnp.where(reset, 0.0, s_t)
        o = _dot(a.astype(BF16), vb[:, sl]) + _dot_nt(qh[:, sl], s_t.astype(BF16))
        v_t = v[:, sl].T.astype(BF16)
        st_t[h] = s_t * dec[:, sl] + _dot(v_t, kh[:, sl])
        ms = jnp.mean(o * o, axis=-1, keepdims=True)
        on = o * lax.rsqrt(ms + LN_EPS) * hgg
        z = h_r[r, C_ZH + h * HEAD_DIM:C_ZH + (h + 1) * HEAD_DIM]
        acth_w[r, sl] = (on * _silu(z)).astype(BF16)


def _prompt_layer_kernel(fence_ref, xn_ref, xp_ref, win_ref, bin_ref, cw_ref, cb_ref, clg_ref, clb_ref, wcp_ref,
                         lb_ref, hgg_ref, whp_ref, wout_ref, lng_ref, lnb_ref,
                         y_ref, convst_ref, st_ref,
                         h0, h1, actc0, actc1, acth0, acth1, u2, c2, g2, yc2, yh2,
                         m2, o2, st_t, xb_scr, xpb_scr, snap_u, snap_st, *, layer, pairs_per_row):
    s = pl.program_id(0)
    zero_u32 = fence_ref[0].astype(jnp.uint32)

    @pl.when(s == 0)
    def _():
        for ref in (h0, h1, actc0, actc1, acth0, acth1, u2, st_t):
            ref[...] = jnp.zeros_like(ref)

    lb = _lower_bounds(lb_ref[...])[layer]
    c0 = 0.5 + 0.5 * lb
    c1 = 0.5 - 0.5 * lb
    hgg = hgg_ref[0]
    ri = lax.broadcasted_iota(jnp.int32, (HG_CHUNK, HG_CHUNK), 0)
    ci = lax.broadcasted_iota(jnp.int32, (HG_CHUNK, HG_CHUNK), 1)
    causal = ri >= ci
    ri3 = lax.broadcasted_iota(jnp.int32, (HG_CHUNK, 3 * HG_CHUNK), 0)
    ci3 = lax.broadcasted_iota(jnp.int32, (HG_CHUNK, 3 * HG_CHUNK), 1)
    tri3 = (ri3 >= (ci3 & (HG_CHUNK - 1))).astype(BF16)
    n_sub = CHUNK // SUB

    def tick(parity, row_start):
        h_w, h_r = (h0, h1) if parity == 0 else (h1, h0)
        g_scr, yc_scr, yh_scr, m_scr, o_scr = (ref.at[parity] for ref in (g2, yc2, yh2, m2, o2))
        actc_w, acth_w = (actc1, acth1) if parity == 0 else (actc0, acth0)
        actc_r, acth_r = (actc0, acth0) if parity == 0 else (actc1, acth1)
        rows = slice(parity * CHUNK, (parity + 1) * CHUNK)

        sub = lambda i: slice(i * SUB, (i + 1) * SUB)
        col = lambda g: slice(g * D_MODEL, (g + 1) * D_MODEL)

        def token(res):
            return pltpu.bitcast(res[0:SUBLANES, 0:LANES], jnp.uint32) & zero_u32

        def in_proj(g):
            res = _dot(xb_scr[...], _unpack(win_ref[0, :, col(g)])) + bin_ref[0, :, col(g)]
            h_w[:, col(g)] = res
            return token(res)

        def glu(i, tok):
            b = _tie(h_r[sub(i), C_GLU_B:C_ZC], tok)
            u = h_r[sub(i), C_GLU_A:C_GLU_B] * _sigmoid(b)
            for k in range(N_SLABS):
                u2[_slab(k, U_PITCH, HALO + i * SUB, SUB), :] = u[:, k * LANES:(k + 1) * LANES]

        def conv(j, tok):
            for t0 in range(2 * j * TOK_BLOCK, 2 * (j + 1) * TOK_BLOCK, TOK_BLOCK):
                last = _conv_tokens(u2, cw_ref, cb_ref, c2, t0, tok | conv_chain[0])
                conv_chain[0] = pltpu.bitcast(last, jnp.uint32) & zero_u32

        def conv_ln(i, tok):
            c = jnp.concatenate([c2[_slab(k, C_PITCH, i * SUB, SUB), :] for k in range(N_SLABS)], axis=1)
            cn = _layer_norm(_tie(c, tok), clg_ref[0], clb_ref[0])
            actc_w[sub(i), :] = (_silu(cn) * _silu(h_r[sub(i), C_ZC:C_Q])).astype(BF16)

        def hgrn(c, tok):
            r = slice(c * HG_CHUNK, (c + 1) * HG_CHUNK)
            _hgrn_chunk(h_r, r, c0, c1, hgg, tri3, causal, st_t, acth_w,
                        row_start if c == 0 else None, tok)

        def gate_proj(g):
            cg = slice(C_MC + g * D_MODEL, C_MC + (g + 1) * D_MODEL)
            res = _dot(xpb_scr[...], _unpack(win_ref[0, :, cg])) + bin_ref[0, :, cg]
            g_scr[:, col(g)] = res
            return token(res)

        def branch_proj(act_ref, w_ref, y_scr):
            res = _dot(act_ref[...], _unpack(w_ref[0]))
            y_scr[...] = res
            return token(res)

        def merge(i):
            m_scr[sub(i), :] = (_sigmoid(g_scr[sub(i), col(0)]) * yc_scr[sub(i), :]
                                + _sigmoid(g_scr[sub(i), col(1)]) * yh_scr[sub(i), :]).astype(BF16)

        def post_ln(i):
            rr = slice(parity * CHUNK + i * SUB, parity * CHUNK + (i + 1) * SUB)
            res = ALPHA * xp_ref[0, rr, :] + o_scr[sub(i), :]
            y_ref[0, rr, :] = _layer_norm(res, lng_ref[0], lnb_ref[0])

        conv_chain = [jnp.zeros((SUBLANES, LANES), jnp.uint32)]

        def load_x():
            xb_scr[...] = xn_ref[0, rows, :].astype(BF16)
            xpb_scr[...] = xp_ref[0, rows, :].astype(BF16)

        def reset_history():
            if row_start is not None:
                for k in range(N_SLABS):
                    hist = _slab(k, U_PITCH, 0, HALO)
                    u2[hist, :] = jnp.where(row_start, 0.0, u2[hist, :])

        def keep_history():
            for k in range(N_SLABS):
                u2[_slab(k, U_PITCH, 0, HALO), :] = u2[_slab(k, U_PITCH, CHUNK, HALO), :]

        def glus(a, b, tok):
            for i in range(a, b):
                glu(i, tok)

        def conv_lns(tok):
            for i in range(n_sub):
                conv_ln(i, tok)

        def merge_out():
            for i in range(n_sub):
                merge(i)
            o_scr[...] = _dot(m_scr[...], _unpack(wout_ref[0]))

        def post_lns():
            for i in range(n_sub):
                post_ln(i)

        return dict(load_x=load_x, reset_history=reset_history, keep_history=keep_history,
                    in_proj=in_proj, gate_proj=gate_proj, glus=glus, conv=conv, conv_lns=conv_lns,
                    hgrn=hgrn, merge_out=merge_out, post_lns=post_lns,
                    proj_c=functools.partial(branch_proj, actc_r, wcp_ref, yc_scr),
                    proj_h=functools.partial(branch_proj, acth_r, whp_ref, yh_scr))

    def run_tick(t):
        t["load_x"]()
        t["reset_history"]()
        tok = t["in_proj"](0); t["glus"](0, 2, tok)
        tok = t["in_proj"](1); t["glus"](2, 4, tok)
        for j in range(5):
            t["conv"](j, t["in_proj"](2 + j))
        t["conv"](5, t["gate_proj"](0))
        t["conv"](6, t["gate_proj"](1))
        t["conv"](7, t["proj_c"]())
        t["keep_history"]()
        tok = t["proj_h"]()
        t["conv_lns"](tok)
        t["hgrn"](0, tok)
        t["merge_out"]()
        t["hgrn"](1, tok)
        t["post_lns"]()

    run_tick(tick(0, None))
    for k in range(N_SLABS):
        snap_u[k * HALO:(k + 1) * HALO, :] = u2[_slab(k, U_PITCH, 0, HALO), :]
    snap_st[...] = st_t[...]
    run_tick(tick(1, s % pairs_per_row == 0))

    @pl.when((s > 0) & (s % pairs_per_row == 0))
    def _():
        for k in range(N_SLABS):
            convst_ref[0, :, k * LANES:(k + 1) * LANES] = snap_u[k * HALO + HALO_SKIP:(k + 1) * HALO, :]
        for h in range(N_HEADS):
            st_ref[0, h] = snap_st[h].T


def _resident(shape, index_map):
    return pl.BlockSpec(shape, index_map, pipeline_mode=pl.Buffered(1))


def _prompt_layer(layer, x, p):
    B, T, _ = x.shape
    ppr = T // PAIR
    n_pairs = B * ppr
    l = layer

    def next_pair(s):
        q = jnp.minimum(s, n_pairs - 1)
        return (q // ppr, q % ppr, 0)

    def prev_pair(s):
        q = jnp.maximum(s - 1, 0)
        return (q // ppr, q % ppr, 0)

    def state_row(s):
        return jnp.clip((2 * s - 1) // (2 * ppr), 0, B - 1)

    vec = lambda n: _resident((1, 1, n), lambda s: (l, 0, 0))
    in_specs = [
        pl.BlockSpec(memory_space=pltpu.SMEM),
        pl.BlockSpec((1, PAIR, D_MODEL), next_pair),
        pl.BlockSpec((1, PAIR, D_MODEL), prev_pair),
        _resident((1, D_MODEL // 2, IN_COLS), lambda s: (l, 0, 0)),
        vec(IN_COLS),
        _resident((1, CONV_WIDTH, N_SLABS, LANES), lambda s: (l, 0, 0, 0)),
        _resident((1, N_SLABS, LANES), lambda s: (l, 0, 0)),
        vec(D_MODEL), vec(D_MODEL),
        _resident((1, D_MODEL // 2, D_MODEL), lambda s: (l, 0, 0)),
        _resident((DEPTH, D_MODEL), lambda s: (0, 0)),
        vec(HEAD_DIM),
        _resident((1, D_MODEL // 2, D_MODEL), lambda s: (l, 0, 0)),
        _resident((1, D_MODEL // 2, D_MODEL), lambda s: (l, 0, 0)),
        vec(D_MODEL), vec(D_MODEL),
    ]
    out_specs = [
        pl.BlockSpec((1, PAIR, D_MODEL), prev_pair),
        pl.BlockSpec((1, CONV_WIDTH - 1, D_MODEL), lambda s: (state_row(s), 0, 0)),
        pl.BlockSpec((1, N_HEADS, HEAD_DIM, HEAD_DIM), lambda s: (state_row(s), 0, 0, 0)),
    ]
    out_shape = [
        jax.ShapeDtypeStruct((B, T, D_MODEL), F32),
        jax.ShapeDtypeStruct((B, CONV_WIDTH - 1, D_MODEL), F32),
        jax.ShapeDtypeStruct((B, N_HEADS, HEAD_DIM, HEAD_DIM), F32),
    ]
    scratch = [
        pltpu.VMEM((CHUNK, C_MC), F32), pltpu.VMEM((CHUNK, C_MC), F32),
        pltpu.VMEM((CHUNK, D_MODEL), BF16), pltpu.VMEM((CHUNK, D_MODEL), BF16),
        pltpu.VMEM((CHUNK, D_MODEL), BF16), pltpu.VMEM((CHUNK, D_MODEL), BF16),
        pltpu.VMEM((N_SLABS * U_PITCH, LANES), F32),
        pltpu.VMEM((N_SLABS * C_PITCH, LANES), F32),
        pltpu.VMEM((2, CHUNK, 2 * D_MODEL), F32),
        pltpu.VMEM((2, CHUNK, D_MODEL), F32), pltpu.VMEM((2, CHUNK, D_MODEL), F32),
        pltpu.VMEM((2, CHUNK, D_MODEL), BF16),
        pltpu.VMEM((2, CHUNK, D_MODEL), F32),
        pltpu.VMEM((N_HEADS, HEAD_DIM, HEAD_DIM), F32),
        pltpu.VMEM((CHUNK, D_MODEL), BF16), pltpu.VMEM((CHUNK, D_MODEL), BF16),
        pltpu.VMEM((N_SLABS * HALO, LANES), F32),
        pltpu.VMEM((N_HEADS, HEAD_DIM, HEAD_DIM), F32),
    ]
    kern = functools.partial(_prompt_layer_kernel, layer=layer, pairs_per_row=ppr)
    return pl.pallas_call(
        kern,
        grid=(n_pairs + 1,),
        in_specs=in_specs,
        out_specs=out_specs,
        out_shape=out_shape,
        scratch_shapes=scratch,
        compiler_params=pltpu.CompilerParams(
            dimension_semantics=("arbitrary",),
            vmem_limit_bytes=VMEM_LIMIT_BYTES),
        name=f"prompt_layer_{layer}",
    )(jnp.zeros((1,), jnp.int32), x, x, p["w_in"], p["b_in"], p["conv_w_tok"], p["conv_b_tok"],
      p["conv_ln_g"], p["conv_ln_b"],
      p["w_conv_proj"], p["lower_bounds"], p["hg_norm_g"], p["w_hg_proj"], p["w_out"],
      p["ln_g"], p["ln_b"])


def _decode_kernel(x_ref, sc_ref, sh_ref, win_ref, bin_ref, cw_ref, cb_ref, clg_ref, clb_ref,
                   wcp_ref, lb_ref, hgg_ref, whp_ref, wout_ref, lng_ref, lnb_ref,
                   y_ref, sco_ref, sho_ref,
                   xcur, hbuf, lbs_scr, q_scr, g_t, k_t, u_scr, o_scr, c_scr, o_stage, c_stage,
                   *, nb, bb):
    l = pl.program_id(0)
    j = pl.program_id(1)
    n = xcur.shape[0]

    @pl.when((l == 0) & (j == 0))
    def _():
        xcur[...] = x_ref[...]
        lbs = _lower_bounds(lb_ref[...])
        lbs_scr[...] = jnp.zeros_like(lbs_scr)
        for i in range(DEPTH):
            lbs_scr[i:i + 1, :] = lbs[i]

    @pl.when(j == 0)
    def _():
        hbuf[...] = _dot(xcur[...].astype(BF16), _unpack(win_ref[0])) + bin_ref[0]
        u_scr[...] = hbuf[:, C_GLU_A:C_GLU_B] * _sigmoid(hbuf[:, C_GLU_B:C_ZC])
        layer_row = lax.broadcasted_iota(jnp.int32, lbs_scr.shape, 0) == l
        lb = jnp.sum(jnp.where(layer_row, lbs_scr[...], 0.0), axis=0, keepdims=True)
        g = lb + (1.0 - lb) * _sigmoid(hbuf[:, C_F:C_I])
        q_scr[...] = _silu(hbuf[:, C_Q:C_F])
        g_t[...] = jnp.maximum(g, G_FLOOR).T
        k_t[...] = (1.0 - g).T

    shift = (n - j * bb) % n
    gc = pltpu.roll(g_t[...], shift, 1)
    kc = pltpu.roll(k_t[...], shift, 1)
    cw_hist = cw_ref[0, 0:CONV_WIDTH - 1, :]
    cw_new = cw_ref[0, CONV_WIDTH - 1:CONV_WIDTH, :]
    rows = _rows(j, bb)
    u_blk = u_scr[rows, :]
    v_blk = hbuf[rows, C_I:C_ZH]
    q_blk = q_scr[rows, :].astype(BF16)
    k_lhs = kc[:, 0:2 * bb].astype(BF16)
    v_pad = jnp.concatenate([v_blk, jnp.zeros_like(v_blk)], axis=0)
    v_row_id = lax.broadcasted_iota(jnp.int32, v_pad.shape, 0)
    for b in range(bb):
        v_only_b = jnp.where(v_row_id == b, v_pad, 0.0).astype(BF16)
        u_row = u_blk[b:b + 1, :]
        hist = sc_ref[0, b]
        c_stage[b:b + 1, :] = (jnp.sum(hist * cw_hist, axis=0, keepdims=True)
                               + u_row * cw_new + cb_ref[0])
        sco_ref[0, b, 0:CONV_WIDTH - 2, :] = sc_ref[0, b, 1:CONV_WIDTH - 1, :]
        sco_ref[0, b, CONV_WIDTH - 2:CONV_WIDTH - 1, :] = u_row
        for h in range(N_HEADS):
            hs = slice(h * HEAD_DIM, (h + 1) * HEAD_DIM)
            s_new = gc[hs, b:b + 1] * sh_ref[0, b, h] + _dot(k_lhs[hs, :], v_only_b[:, hs])
            sho_ref[0, b, h] = s_new
            o_stage[b:b + 1, hs] = _dot(q_blk[:, hs], s_new.astype(BF16))[b:b + 1, :]
    c_scr[rows, :] = c_stage[...]
    o_scr[rows, :] = o_stage[...]

    @pl.when(j == nb - 1)
    def _():
        cn = _layer_norm(c_scr[...], clg_ref[0], clb_ref[0])
        ca = (_silu(cn) * _silu(hbuf[:, C_ZC:C_Q])).astype(BF16)
        yc = _dot(ca, _unpack(wcp_ref[0]))
        for h in range(N_HEADS):
            hs = slice(h * HEAD_DIM, (h + 1) * HEAD_DIM)
            o = o_scr[:, hs]
            ms = jnp.mean(o * o, axis=-1, keepdims=True)
            on = o * lax.rsqrt(ms + LN_EPS) * hgg_ref[0]
            o_scr[:, hs] = on * _silu(hbuf[:, C_ZH + h * HEAD_DIM:C_ZH + (h + 1) * HEAD_DIM])
        yh = _dot(o_scr[...].astype(BF16), _unpack(whp_ref[0]))
        merged = _sigmoid(hbuf[:, C_MC:C_MH]) * yc + _sigmoid(hbuf[:, C_MH:IN_COLS]) * yh
        out = _dot(merged.astype(BF16), _unpack(wout_ref[0]))
        x_new = _layer_norm(ALPHA * xcur[...] + out, lng_ref[0], lnb_ref[0])
        xcur[...] = x_new
        y_ref[...] = x_new


def _decode(x, state_conv, state_hgrn, p):
    n = x.shape[0]
    bb = DEC_BLOCK
    nb = n // bb
    vec = lambda m: _resident((1, 1, m), lambda l, j: (l, 0, 0))
    in_specs = [
        _resident((n, D_MODEL), lambda l, j: (0, 0)),
        pl.BlockSpec((1, bb, CONV_WIDTH - 1, D_MODEL), lambda l, j: (l, j, 0, 0)),
        pl.BlockSpec((1, bb, N_HEADS, HEAD_DIM, HEAD_DIM), lambda l, j: (l, j, 0, 0, 0)),
        _resident((1, D_MODEL // 2, IN_COLS), lambda l, j: (l, 0, 0)),
        vec(IN_COLS),
        _resident((1, CONV_WIDTH, D_MODEL), lambda l, j: (l, 0, 0)),
        vec(D_MODEL), vec(D_MODEL), vec(D_MODEL),
        _resident((1, D_MODEL // 2, D_MODEL), lambda l, j: (l, 0, 0)),
        _resident((DEPTH, D_MODEL), lambda l, j: (0, 0)),
        vec(HEAD_DIM),
        _resident((1, D_MODEL // 2, D_MODEL), lambda l, j: (l, 0, 0)),
        _resident((1, D_MODEL // 2, D_MODEL), lambda l, j: (l, 0, 0)),
        vec(D_MODEL), vec(D_MODEL),
    ]
    out_specs = [
        pl.BlockSpec((n, D_MODEL), lambda l, j: (0, 0)),
        pl.BlockSpec((1, bb, CONV_WIDTH - 1, D_MODEL), lambda l, j: (l, j, 0, 0)),
        pl.BlockSpec((1, bb, N_HEADS, HEAD_DIM, HEAD_DIM), lambda l, j: (l, j, 0, 0, 0)),
    ]
    out_shape = [
        jax.ShapeDtypeStruct((n, D_MODEL), F32),
        jax.ShapeDtypeStruct(state_conv.shape, F32),
        jax.ShapeDtypeStruct(state_hgrn.shape, F32),
    ]
    scratch = [
        pltpu.VMEM((n, D_MODEL), F32),
        pltpu.VMEM((n, IN_COLS), F32),
        pltpu.VMEM((8, D_MODEL), F32),
        pltpu.VMEM((n, D_MODEL), F32),
        pltpu.VMEM((D_MODEL, n), F32),
        pltpu.VMEM((D_MODEL, n), F32),
        pltpu.VMEM((n, D_MODEL), F32),
        pltpu.VMEM((n, D_MODEL), F32),
        pltpu.VMEM((n, D_MODEL), F32),
        pltpu.VMEM((bb, D_MODEL), F32),
        pltpu.VMEM((bb, D_MODEL), F32),
    ]
    kern = functools.partial(_decode_kernel, nb=nb, bb=bb)
    return pl.pallas_call(
        kern,
        grid=(DEPTH, nb),
        in_specs=in_specs,
        out_specs=out_specs,
        out_shape=out_shape,
        scratch_shapes=scratch,
        compiler_params=pltpu.CompilerParams(
            dimension_semantics=("arbitrary", "arbitrary"),
            vmem_limit_bytes=VMEM_LIMIT_BYTES),
        name="decode_layers",
    )(x, state_conv, state_hgrn, p["w_in"], p["b_in"], p["conv_w"], p["conv_b"], p["conv_ln_g"],
      p["conv_ln_b"], p["w_conv_proj"], p["lower_bounds"], p["hg_norm_g"], p["w_hg_proj"],
      p["w_out"], p["ln_g"], p["ln_b"])


def kernel(x_prompt, x_sample, state_conv, state_hgrn, w_in, b_in, conv_w, conv_b, conv_ln_g,
           conv_ln_b, w_conv_proj, lower_bounds, hg_norm_g, w_hg_proj, w_out, ln_g, ln_b):
    row = lambda a: a.reshape(DEPTH, 1, a.shape[-1])
    p = dict(
        w_in=_pack_rows(w_in), b_in=row(b_in), conv_w=conv_w, conv_b=row(conv_b),
        conv_ln_g=row(conv_ln_g), conv_ln_b=row(conv_ln_b), w_conv_proj=_pack_rows(w_conv_proj),
        lower_bounds=lower_bounds, hg_norm_g=row(hg_norm_g), w_hg_proj=_pack_rows(w_hg_proj),
        w_out=_pack_rows(w_out), ln_g=row(ln_g), ln_b=row(ln_b),
        conv_w_tok=conv_w.reshape(DEPTH, CONV_WIDTH, N_SLABS, LANES),
        conv_b_tok=conv_b.reshape(DEPTH, N_SLABS, LANES))

    xp = x_prompt
    conv_p, hgrn_p = [], []
    for layer in range(DEPTH):
        xp, cs, hs = _prompt_layer(layer, xp, p)
        conv_p.append(cs)
        hgrn_p.append(hs)

    n = x_sample.shape[0]
    ys, conv_s, hgrn_s = _decode(x_sample.reshape(n, D_MODEL), state_conv, state_hgrn, p)
    return (xp, ys.reshape(x_sample.shape), jnp.stack(conv_p), jnp.stack(hgrn_p), conv_s, hgrn_s)
```
